```python
import math
import jax, jax.numpy as jnp
from jax import lax
import numpy as np

D_MODEL = 1024
BATCH = 32
SEQ = 2048
DEPTH = 1
DEC_BATCH = 128
DEC_SEQ = 1
PAST_LEN = 16384
PAGE_SIZE = 128

A_HEADS = 8
A_KV_HEADS = 4
A_HEAD_DIM = 64
IDX_HEADS = 8
IDX_DIM = 64
TOPK_MAX = 256
INDEX_SCALE = (IDX_HEADS ** -0.5) * (IDX_DIM ** -0.5)
M_HEADS = 8
M_Q_RANK = 384
M_KV_RANK = 256
M_NOPE = 64
M_ROPE = 32
M_V = 64
MLA_SCALE = (M_NOPE + M_ROPE) ** -0.5
ROPE_BASE = 10000.0
REL_BUCKETS = 32
REL_MAX_DIST = 128
N_GROUPS = 4
EXPERTS_PER_GROUP = 8
N_EXPERTS = 32
TOPK_IN_GROUP = 2
D_EXPERT = 256
MOE_BLOCK = 128
Q_BLOCK = 128
EPS = 1e-6
A_WIDTH = A_HEADS * A_HEAD_DIM
M_WIDTH = M_HEADS * M_V
IN_SPLITS = (A_HEADS * A_HEAD_DIM, A_KV_HEADS * A_HEAD_DIM, A_KV_HEADS * A_HEAD_DIM, IDX_HEADS * IDX_DIM, IDX_HEADS, IDX_DIM, M_Q_RANK, M_KV_RANK, M_ROPE, D_MODEL, D_MODEL)
IN_WIDTH = sum(IN_SPLITS)

kernel_name = "hybrid_dsa_mla_hmoe_step"


def rms_norm(x, g):
    xf = x.astype(jnp.float32)
    y = xf * lax.rsqrt(jnp.mean(xf * xf, axis=-1, keepdims=True) + EPS)
    return (y * g.astype(jnp.float32)).astype(x.dtype)


def rope(x, pos):
    half = x.shape[-1] // 2
    inv = ROPE_BASE ** (-jnp.arange(half, dtype=jnp.float32) / half)
    ang = pos.astype(jnp.float32)[..., None] * inv
    cos, sin = jnp.cos(ang), jnp.sin(ang)
    x1 = x[..., :half].astype(jnp.float32)
    x2 = x[..., half:].astype(jnp.float32)
    return jnp.concatenate([x1 * cos - x2 * sin, x1 * sin + x2 * cos], axis=-1).astype(x.dtype)


def rel_bucket(dist):
    n = jnp.maximum(dist, 0)
    max_exact = REL_BUCKETS // 2
    nf = jnp.maximum(n, 1).astype(jnp.float32)
    large = max_exact + (jnp.log(nf / max_exact) / math.log(REL_MAX_DIST / max_exact) * (REL_BUCKETS - max_exact)).astype(jnp.int32)
    large = jnp.minimum(large, REL_BUCKETS - 1)
    return jnp.where(n < max_exact, n, large)


def indexer_scores(qi, wi, ki):
    dots = jnp.einsum('bthd,bsd->bths', qi, ki).astype(jnp.float32)
    return jnp.einsum('bths,bth->bts', jax.nn.relu(dots), wi.astype(jnp.float32))


def sparse_attend(q, k_sel, v_sel, q_pos, sel_pos, rel_bias):
    B, T, H, hd = q.shape
    G = H // A_KV_HEADS
    qg = q.reshape(B, T, A_KV_HEADS, G, hd)
    s = jnp.einsum('btkgd,btskd->btkgs', qg, k_sel).astype(jnp.float32) * (hd ** -0.5)
    dist = q_pos - sel_pos
    bias = rel_bias[rel_bucket(dist)]
    bias = jnp.moveaxis(bias, -1, 2).reshape(B, T, A_KV_HEADS, G, -1)
    s = s + bias.astype(jnp.float32)
    s = jnp.where((dist >= 0)[:, :, None, None, :], s, -jnp.inf)
    p = jax.nn.softmax(s, axis=-1)
    o = jnp.einsum('btkgs,btskd->btkgd', p.astype(v_sel.dtype), v_sel)
    return o.reshape(B, T, H * hd)


def dsa_prompt(q, k, v, qi, wi, ki, rel_bias):
    B, S = q.shape[:2]
    topk = min(TOPK_MAX, S // 4)
    nb = S // Q_BLOCK
    key_pos = jnp.arange(S, dtype=jnp.int32)

    def blocks(a):
        return jnp.moveaxis(a.reshape((B, nb, Q_BLOCK) + a.shape[2:]), 1, 0)

    def one_block(args):
        qb, qib, wib, start = args
        q_pos = start + jnp.arange(Q_BLOCK, dtype=jnp.int32)
        sc = indexer_scores(qib, wib, ki)
        sc = jnp.where(key_pos[None, None, :] <= q_pos[None, :, None], sc, -jnp.inf)
        _, sel = lax.top_k(sc, topk)
        k_sel = jax.vmap(lambda kb, ib: kb[ib])(k, sel)
        v_sel = jax.vmap(lambda vb, ib: vb[ib])(v, sel)
        return sparse_attend(qb, k_sel, v_sel, q_pos[None, :, None], sel, rel_bias)

    starts = jnp.arange(nb, dtype=jnp.int32) * Q_BLOCK
    out = lax.map(one_block, (blocks(q), blocks(qi), blocks(wi), starts))
    return jnp.moveaxis(out, 0, 1).reshape(B, S, -1)


def dsa_sample(q, k_new, v_new, qi, wi, ki_new, cache_k, cache_v, cache_ki, page_table, rel_bias, layer):
    DB, T = q.shape[:2]
    past = page_table.shape[1] * PAGE_SIZE
    L = past + T
    topk = min(TOPK_MAX, L // 4)
    ki_past = cache_ki[layer, page_table].reshape(DB, past, IDX_DIM)
    ki_all = jnp.concatenate([ki_past, ki_new.astype(ki_past.dtype)], axis=1)
    q_pos = past + jnp.arange(T, dtype=jnp.int32)
    sc = indexer_scores(qi, wi, ki_all)
    sc = jnp.where(jnp.arange(L, dtype=jnp.int32)[None, None, :] <= q_pos[None, :, None], sc, -jnp.inf)
    _, sel = lax.top_k(sc, topk)
    is_past = sel < past
    ps = jnp.minimum(sel, past - 1)
    phys = jax.vmap(lambda pt, i: pt[i])(page_table, ps // PAGE_SIZE)
    slot = ps % PAGE_SIZE
    nidx = jnp.clip(sel - past, 0, T - 1)
    k_past = cache_k[layer, phys, slot]
    v_past = cache_v[layer, phys, slot]
    k_cur = jax.vmap(lambda kb, ib: kb[ib])(k_new, nidx)
    v_cur = jax.vmap(lambda vb, ib: vb[ib])(v_new, nidx)
    m = is_past[..., None, None]
    k_sel = jnp.where(m, k_past.astype(k_cur.dtype), k_cur)
    v_sel = jnp.where(m, v_past.astype(v_cur.dtype), v_cur)
    return sparse_attend(q, k_sel, v_sel, q_pos[None, :, None], sel, rel_bias)


def mla_expand_k(c_kv, w_uk, g_kn):
    return rms_norm(jnp.einsum('bsc,chd->bshd', c_kv, w_uk), g_kn)


def mla_prompt(q_nope, q_rope, c_kv, k_rope, w_uk, w_uv, g_kn):
    B, S = q_nope.shape[:2]
    nb = S // Q_BLOCK
    k_nope = mla_expand_k(c_kv, w_uk, g_kn)
    v = jnp.einsum('bsc,chd->bshd', c_kv, w_uv)
    key_pos = jnp.arange(S, dtype=jnp.int32)

    def blocks(a):
        return jnp.moveaxis(a.reshape((B, nb, Q_BLOCK) + a.shape[2:]), 1, 0)

    def one_block(args):
        qn, qr, start = args
        q_pos = start + jnp.arange(Q_BLOCK, dtype=jnp.int32)
        s = (jnp.einsum('bthd,bshd->bhts', qn, k_nope) + jnp.einsum('bthr,bsr->bhts', qr, k_rope)).astype(jnp.float32) * MLA_SCALE
        s = jnp.where(key_pos[None, None, None, :] <= q_pos[None, None, :, None], s, -jnp.inf)
        p = jax.nn.softmax(s, axis=-1)
        return jnp.einsum('bhts,bshd->bthd', p.astype(v.dtype), v)

    starts = jnp.arange(nb, dtype=jnp.int32) * Q_BLOCK
    out = lax.map(one_block, (blocks(q_nope), blocks(q_rope), starts))
    return jnp.moveaxis(out, 0, 1).reshape(B, S, M_HEADS * M_V)


def mla_sample(q_nope, q_rope, c_new, kr_new, cache_lat, cache_kr, page_table, w_uk, w_uv, g_kn, layer):
    DB, T = q_nope.shape[:2]

    def scores(lat, kr):
        kn = mla_expand_k(lat, w_uk, g_kn)
        return (jnp.einsum('bthd,bshd->bths', q_nope, kn) + jnp.einsum('bthr,bsr->bths', q_rope, kr)).astype(jnp.float32) * MLA_SCALE

    s_new = scores(c_new, kr_new)
    causal = jnp.arange(T)[:, None] >= jnp.arange(T)[None, :]
    s_new = jnp.where(causal[None, :, None, :], s_new, -jnp.inf)
    m0 = jnp.max(s_new, axis=-1)
    p0 = jnp.exp(s_new - m0[..., None])
    l0 = jnp.sum(p0, axis=-1)
    acc0 = jnp.einsum('bths,bsc->bthc', p0, c_new.astype(jnp.float32))

    def step(carry, phys):
        m, l, acc = carry
        lat = cache_lat[layer, phys].astype(c_new.dtype)
        kr = cache_kr[layer, phys].astype(kr_new.dtype)
        s = scores(lat, kr)
        m_new = jnp.maximum(m, jnp.max(s, axis=-1))
        corr = jnp.exp(m - m_new)
        p = jnp.exp(s - m_new[..., None])
        l = l * corr + jnp.sum(p, axis=-1)
        acc = acc * corr[..., None] + jnp.einsum('bths,bsc->bthc', p, lat.astype(jnp.float32))
        return (m_new, l, acc), None

    (m, l, acc), _ = lax.scan(step, (m0, l0, acc0), page_table.T)
    lat_out = (acc / l[..., None]).astype(c_new.dtype)
    out = jnp.einsum('bthc,chd->bthd', lat_out, w_uv)
    return out.reshape(DB, T, M_HEADS * M_V)


def mixer_inputs(x, pos, lw):
    B, T, _ = x.shape
    xn = rms_norm(x, lw['g_attn'])
    z = xn @ lw['w_in']
    offs = [int(o) for o in np.cumsum(IN_SPLITS)[:-1]]
    aq, ak, av, iq, iw, ik, mq, mkv, mkr, ga, gb = jnp.split(z, offs, axis=-1)
    out = {}
    out['a_q'] = rms_norm(aq.reshape(B, T, A_HEADS, A_HEAD_DIM), lw['g_a_q'])
    out['a_k'] = rms_norm(ak.reshape(B, T, A_KV_HEADS, A_HEAD_DIM), lw['g_a_k'])
    out['a_v'] = av.reshape(B, T, A_KV_HEADS, A_HEAD_DIM)
    out['i_q'] = iq.reshape(B, T, IDX_HEADS, IDX_DIM)
    out['i_w'] = iw * INDEX_SCALE
    out['i_k'] = ik
    c_q = rms_norm(mq, lw['g_m_qa'])
    qm = (c_q @ lw['w_uq']).reshape(B, T, M_HEADS, M_NOPE + M_ROPE)
    out['q_nope'] = rms_norm(qm[..., :M_NOPE], lw['g_m_qn'])
    out['q_rope'] = rope(rms_norm(qm[..., M_NOPE:], lw['g_m_qr']), pos[:, None])
    out['c_kv'] = rms_norm(mkv, lw['g_m_kva'])
    out['k_rope'] = rope(rms_norm(mkr, lw['g_m_kr']), pos)
    b_gate = lw['b_gate']
    out['gate_a'] = jax.nn.sigmoid(ga.astype(jnp.float32) + b_gate[0].astype(jnp.float32)).astype(x.dtype)
    out['gate_b'] = jax.nn.sigmoid(gb.astype(jnp.float32) + b_gate[1].astype(jnp.float32)).astype(x.dtype)
    return out


def expert_dispatch(x, eid, wts, w1, w3, w2):
    T, K = eid.shape
    D = x.shape[-1]
    A = T * K
    n_blocks = -(-A // MOE_BLOCK) + N_EXPERTS
    P = n_blocks * MOE_BLOCK
    flat_e = eid.reshape(A)
    flat_tok = jnp.repeat(jnp.arange(T, dtype=jnp.int32), K)
    order = jnp.argsort(flat_e)
    se = flat_e[order]
    stok = flat_tok[order]
    sw = wts.reshape(A)[order]
    counts = jnp.bincount(flat_e, length=N_EXPERTS)
    padded = (counts + MOE_BLOCK - 1) // MOE_BLOCK * MOE_BLOCK
    pad_end = jnp.cumsum(padded)
    pad_start = pad_end - padded
    start = jnp.cumsum(counts) - counts
    dest = pad_start[se] + jnp.arange(A, dtype=jnp.int32) - start[se]
    buf_tok = jnp.zeros((P,), jnp.int32).at[dest].set(stok)
    blk_e = jnp.minimum(jnp.searchsorted(pad_end, jnp.arange(n_blocks, dtype=pad_end.dtype) * MOE_BLOCK, side='right'), N_EXPERTS - 1)
    xb = x[buf_tok].reshape(n_blocks, MOE_BLOCK, D)

    def one(args):
        xblk, e = args
        h = jax.nn.silu(xblk @ w1[e]) * (xblk @ w3[e])
        return h @ w2[e]

    yb = lax.map(one, (xb, blk_e)).reshape(P, D)
    return jnp.zeros_like(x).at[stok].add(yb[dest] * sw[:, None].astype(x.dtype))


def hier_moe(x, lw):
    T = x.shape[0]
    gl = (x @ lw['w_group'] + lw['b_group']).astype(jnp.float32)
    pg = jax.nn.softmax(gl, axis=-1)
    pg_top, g_top = lax.top_k(pg, 1)
    el = (x @ lw['w_expert'] + lw['b_expert']).astype(jnp.float32).reshape(T, N_GROUPS, EXPERTS_PER_GROUP)
    el_g = el[jnp.arange(T), g_top[:, 0]]
    tl, ti = lax.top_k(el_g, TOPK_IN_GROUP)
    wts = jax.nn.softmax(tl, axis=-1) * pg_top
    eid = g_top * EXPERTS_PER_GROUP + ti
    return expert_dispatch(x, eid, wts, lw['w_e1'], lw['w_e3'], lw['w_e2'])


def merge_and_ffn(x, y_a, y_b, gate_a, gate_b, lw):
    merged = gate_a * (y_a @ lw['w_oa']) + gate_b * (y_b @ lw['w_ob'])
    h = x + merged @ lw['w_out']
    B, T, D = h.shape
    hn = rms_norm(h, lw['g_ffn'])
    return h + hier_moe(hn.reshape(B * T, D), lw).reshape(B, T, D)


def setup_inputs(seed: int = 0) -> dict:
    key = jax.random.key(seed)
    k = jax.random.split(key, 40)
    f32 = jnp.float32
    n_pages = PAST_LEN // PAGE_SIZE
    n_pool = (DEC_BATCH * n_pages * 5) // 4

    def nrm(kk, shape, scale):
        return scale * jax.random.normal(kk, shape, f32)

    def gain(kk, shape):
        return 1.0 + 0.05 * jax.random.normal(kk, shape, f32)

    page_table = jax.random.permutation(k[7], n_pool)[:DEC_BATCH * n_pages].reshape(DEC_BATCH, n_pages).astype(jnp.int32)
    return {
        'x_prompt': nrm(k[0], (BATCH, SEQ, D_MODEL), 1.0),
        'x_sample': nrm(k[1], (DEC_BATCH, DEC_SEQ, D_MODEL), 1.0),
        'cache_a_k': nrm(k[2], (DEPTH, n_pool, PAGE_SIZE, A_KV_HEADS, A_HEAD_DIM), 1.0),
        'cache_a_v': nrm(k[3], (DEPTH, n_pool, PAGE_SIZE, A_KV_HEADS, A_HEAD_DIM), 1.0),
        'cache_idx_k': nrm(k[4], (DEPTH, n_pool, PAGE_SIZE, IDX_DIM), 1.0),
        'cache_mla_latent': nrm(k[5], (DEPTH, n_pool, PAGE_SIZE, M_KV_RANK), 1.0),
        'cache_mla_krope': nrm(k[6], (DEPTH, n_pool, PAGE_SIZE, M_ROPE), 1.0),
        'page_table': page_table,
        'rel_bias': nrm(k[8], (REL_BUCKETS, A_HEADS), 0.5),
        'g_attn': gain(k[9], (DEPTH, D_MODEL)),
        'w_in': nrm(k[10], (DEPTH, D_MODEL, IN_WIDTH), D_MODEL ** -0.5),
        'b_gate': nrm(k[11], (DEPTH, 2, D_MODEL), 0.1),
        'g_a_q': gain(k[12], (DEPTH, A_HEAD_DIM)),
        'g_a_k': gain(k[13], (DEPTH, A_HEAD_DIM)),
        'g_m_qa': gain(k[14], (DEPTH, M_Q_RANK)),
        'w_uq': nrm(k[15], (DEPTH, M_Q_RANK, M_HEADS * (M_NOPE + M_ROPE)), M_Q_RANK ** -0.5),
        'g_m_qn': gain(k[16], (DEPTH, M_NOPE)),
        'g_m_qr': gain(k[17], (DEPTH, M_ROPE)),
        'g_m_kva': gain(k[18], (DEPTH, M_KV_RANK)),
        'g_m_kr': gain(k[19], (DEPTH, M_ROPE)),
        'w_uk': nrm(k[20], (DEPTH, M_KV_RANK, M_HEADS, M_NOPE), M_KV_RANK ** -0.5),
        'g_m_kn': gain(k[21], (DEPTH, M_NOPE)),
        'w_uv': nrm(k[22], (DEPTH, M_KV_RANK, M_HEADS, M_V), M_KV_RANK ** -0.5),
        'w_oa': nrm(k[23], (DEPTH, A_WIDTH, D_MODEL), A_WIDTH ** -0.5),
        'w_ob': nrm(k[24], (DEPTH, M_WIDTH, D_MODEL), M_WIDTH ** -0.5),
        'w_out': nrm(k[25], (DEPTH, D_MODEL, D_MODEL), D_MODEL ** -0.5),
        'g_ffn': gain(k[26], (DEPTH, D_MODEL)),
        'w_group': nrm(k[27], (DEPTH, D_MODEL, N_GROUPS), D_MODEL ** -0.5),
        'b_group': nrm(k[28], (DEPTH, N_GROUPS), 0.01),
        'w_expert': nrm(k[29], (DEPTH, D_MODEL, N_EXPERTS), D_MODEL ** -0.5),
        'b_expert': nrm(k[30], (DEPTH, N_EXPERTS), 0.01),
        'w_e1': nrm(k[31], (DEPTH, N_EXPERTS, D_MODEL, D_EXPERT), D_MODEL ** -0.5),
        'w_e3': nrm(k[32], (DEPTH, N_EXPERTS, D_MODEL, D_EXPERT), D_MODEL ** -0.5),
        'w_e2': nrm(k[33], (DEPTH, N_EXPERTS, D_EXPERT, D_MODEL), D_EXPERT ** -0.5),
    }


def reference(x_prompt, x_sample, cache_a_k, cache_a_v, cache_idx_k, cache_mla_latent, cache_mla_krope, page_table, rel_bias, g_attn, w_in, b_gate, g_a_q, g_a_k, g_m_qa, w_uq, g_m_qn, g_m_qr, g_m_kva, g_m_kr, w_uk, g_m_kn, w_uv, w_oa, w_ob, w_out, g_ffn, w_group, b_group, w_expert, b_expert, w_e1, w_e3, w_e2):
    S = x_prompt.shape[1]
    T = x_sample.shape[1]
    past = page_table.shape[1] * PAGE_SIZE
    pos_p = jnp.arange(S, dtype=jnp.int32)
    pos_s = past + jnp.arange(T, dtype=jnp.int32)
    hp, hs = x_prompt, x_sample
    pa_k, pa_v, pi_k, pm_c, pm_r = [], [], [], [], []
    sa_k, sa_v, si_k, sm_c, sm_r = [], [], [], [], []
    for l in range(DEPTH):
        lw = dict(g_attn=g_attn[l], w_in=w_in[l], b_gate=b_gate[l], g_a_q=g_a_q[l], g_a_k=g_a_k[l],
                  g_m_qa=g_m_qa[l], w_uq=w_uq[l], g_m_qn=g_m_qn[l], g_m_qr=g_m_qr[l], g_m_kva=g_m_kva[l],
                  g_m_kr=g_m_kr[l], w_oa=w_oa[l], w_ob=w_ob[l], w_out=w_out[l], g_ffn=g_ffn[l],
                  w_group=w_group[l], b_group=b_group[l], w_expert=w_expert[l], b_expert=b_expert[l],
                  w_e1=w_e1[l], w_e3=w_e3[l], w_e2=w_e2[l])
        mp = mixer_inputs(hp, pos_p, lw)
        ms = mixer_inputs(hs, pos_s, lw)
        ya_p = dsa_prompt(mp['a_q'], mp['a_k'], mp['a_v'], mp['i_q'], mp['i_w'], mp['i_k'], rel_bias)
        yb_p = mla_prompt(mp['q_nope'], mp['q_rope'], mp['c_kv'], mp['k_rope'], w_uk[l], w_uv[l], g_m_kn[l])
        ya_s = dsa_sample(ms['a_q'], ms['a_k'], ms['a_v'], ms['i_q'], ms['i_w'], ms['i_k'], cache_a_k, cache_a_v, cache_idx_k, page_table, rel_bias, l)
        yb_s = mla_sample(ms['q_nope'], ms['q_rope'], ms['c_kv'], ms['k_rope'], cache_mla_latent, cache_mla_krope, page_table, w_uk[l], w_uv[l], g_m_kn[l], l)
        pa_k.append(mp['a_k']); pa_v.append(mp['a_v']); pi_k.append(mp['i_k']); pm_c.append(mp['c_kv']); pm_r.append(mp['k_rope'])
        sa_k.append(ms['a_k']); sa_v.append(ms['a_v']); si_k.append(ms['i_k']); sm_c.append(ms['c_kv']); sm_r.append(ms['k_rope'])
        hp = merge_and_ffn(hp, ya_p, yb_p, mp['gate_a'], mp['gate_b'], lw)
        hs = merge_and_ffn(hs, ya_s, yb_s, ms['gate_a'], ms['gate_b'], lw)
    return (hp, hs, jnp.stack(pa_k), jnp.stack(pa_v), jnp.stack(pi_k), jnp.stack(pm_c), jnp.stack(pm_r), jnp.stack(sa_k), jnp.stack(sa_v), jnp.stack(si_k), jnp.stack(sm_c), jnp.stack(sm_r))
```

```python
import functools
import math

import jax
import jax.numpy as jnp
import numpy as np
from jax import lax
from jax.experimental import pallas as pl
from jax.experimental.pallas import tpu as pltpu

F32 = jnp.float32
BF16 = jnp.bfloat16
I32 = jnp.int32

LANES = 128
EPS = 1e-6
A_HEADS = 8
A_KV_HEADS = 4
A_HEAD_DIM = 64
IDX_HEADS = 8
IDX_DIM = 64
TOPK_MAX = 256
INDEX_SCALE = (IDX_HEADS ** -0.5) * (IDX_DIM ** -0.5)
M_HEADS = 8
M_Q_RANK = 384
M_KV_RANK = 256
M_NOPE = 64
M_ROPE = 32
M_V = 64
MLA_SCALE = (M_NOPE + M_ROPE) ** -0.5
ROPE_BASE = 10000.0
REL_BUCKETS = 32
REL_MAX_DIST = 128
N_GROUPS = 4
EXPERTS_PER_GROUP = 8
N_EXPERTS = 32
D_EXPERT = 256
MOE_BLOCK = 128
Q_BLOCK = 128
PAGE_SIZE = 128
IN_SPLITS = (512, 256, 256, 512, 8, 64, 384, 256, 32, 1024, 1024)
NEG_INF = float("-inf")
VMEM_LIMIT = 56 * 1024 * 1024


def _dot(a, b):
    return jnp.dot(a, b, preferred_element_type=F32)


def _dot_nt(a, b):
    return lax.dot_general(a, b, (((1,), (1,)), ((), ())), preferred_element_type=F32)


def _cparams(sem):
    return pltpu.CompilerParams(dimension_semantics=sem, vmem_limit_bytes=VMEM_LIMIT)


def _expand_heads(z):
    lane = lax.broadcasted_iota(I32, (z.shape[0], LANES), 1)
    lo = lane < 64
    parts = []
    for j in range(z.shape[1] // LANES):
        c = z[:, j * LANES:(j + 1) * LANES]
        parts.append(jnp.where(lo, c, 0.0))
        parts.append(jnp.where(lo, pltpu.roll(c, 64, 1), 0.0))
    return jnp.concatenate(parts, axis=1)


def _rope_slot(r, c_tab, s1_tab, s2_tab):
    return r * c_tab + pltpu.roll(r, LANES - 16, 1) * s1_tab + pltpu.roll(r, 16, 1) * s2_tab


def _proj_kernel(x_ref, ctab_ref, s1tab_ref, s2tab_ref, gattn_ref, w_ref, bgate_ref, e64_ref,
                 gaq_ref, gak_ref, gqa_ref, wuq_ref, gqn_ref, gqr_ref, gkva_ref, gkr_ref,
                 wuk_ref, gkn_ref, wuv_ref,
                 ak_o, av_o, ik_o, ckv_o, kr_o,
                 aqp_o, akp_o, avt_o, iqp_o, iwt_o, ikb_o, qn_o, qr_o, kp_o, vmt_o, ga_o, gb_o):
    x = x_ref[0]
    ms = jnp.mean(x * x, axis=-1, keepdims=True)
    xn = x * lax.rsqrt(ms + EPS) * gattn_ref[...]
    xb = xn.astype(BF16)
    ctab, s1tab, s2tab = ctab_ref[...], s1tab_ref[...], s2tab_ref[...]

    def head_norm(z, width):
        ss = _dot((z * z).astype(BF16), e64_ref[:width, :width])
        return z * lax.rsqrt(ss * (1.0 / 64.0) + EPS)

    z = _dot(xb, w_ref[:, 0:512])
    zn = head_norm(z, 512) * gaq_ref[...] * (A_HEAD_DIM ** -0.5)
    aqp_o[0] = _expand_heads(zn).astype(BF16)
    z = _dot(xb, w_ref[:, 512:768])
    zn = head_norm(z, 256) * gak_ref[...]
    ak_o[0] = zn
    akp_o[0] = _expand_heads(zn).astype(BF16)
    z = _dot(xb, w_ref[:, 768:1024])
    av_o[0] = z
    avt_o[0, 0] = z.T.astype(BF16)
    z = _dot(xb, w_ref[:, 1024:1536])
    iqp_o[0] = _expand_heads(z).astype(BF16)
    z = _dot(xb, w_ref[:, 1536:1664]) * INDEX_SCALE
    iwt_o[0] = z.T[0:IDX_HEADS, :]
    z = _dot(xb, w_ref[:, 1664:1792])
    ik_o[0] = z[:, 0:IDX_DIM]
    ikb_o[0] = z.astype(BF16)
    z = _dot(xb, w_ref[:, 1792:2176])
    cq = z * lax.rsqrt(jnp.mean(z * z, axis=-1, keepdims=True) + EPS) * gqa_ref[...]
    cqb = cq.astype(BF16)
    z = _dot(cqb, wuq_ref[:, 0:512])
    qn_o[0] = (head_norm(z, 512) * gqn_ref[...] * MLA_SCALE).astype(BF16)
    parts = []
    for h in range(M_HEADS):
        r = _dot(cqb, wuq_ref[:, 512 + h * LANES:512 + (h + 1) * LANES])
        ss = jnp.sum(r * r, axis=-1, keepdims=True) * (1.0 / M_ROPE)
        rn = r * lax.rsqrt(ss + EPS) * gqr_ref[...]
        parts.append((_rope_slot(rn, ctab, s1tab, s2tab) * MLA_SCALE).astype(BF16))
    qr_o[0] = jnp.concatenate(parts, axis=1)
    z = _dot(xb, w_ref[:, 2176:2432])
    ckv = z * lax.rsqrt(jnp.mean(z * z, axis=-1, keepdims=True) + EPS) * gkva_ref[...]
    ckv_o[0] = ckv
    cb = ckv.astype(BF16)
    kn = _dot(cb, wuk_ref[...])
    knb = (head_norm(kn, 512) * gkn_ref[...]).astype(BF16)
    vmt_o[0, 0] = _dot(cb, wuv_ref[...]).T.astype(BF16)
    z = _dot(xb, w_ref[:, 2432:2560])
    ss = jnp.sum(z * z, axis=-1, keepdims=True) * (1.0 / M_ROPE)
    kr = _rope_slot(z * lax.rsqrt(ss + EPS) * gkr_ref[...], ctab, s1tab, s2tab)
    kr_o[0] = kr[:, 0:M_ROPE]
    krb = kr.astype(BF16)
    kp_o[0] = jnp.concatenate([p for j in range(4) for p in (knb[:, j * LANES:(j + 1) * LANES], krb)], axis=1)
    z = _dot(xb, w_ref[:, 2560:3584]) + bgate_ref[0:1, :]
    ga_o[0] = jax.nn.sigmoid(z).astype(BF16)
    z = _dot(xb, w_ref[:, 3584:4608]) + bgate_ref[1:2, :]
    gb_o[0] = jax.nn.sigmoid(z).astype(BF16)


def _proj_weights(g_attn, w_in, b_gate, g_a_q, g_a_k, g_m_qa, w_uq, g_m_qn, g_m_qr, g_m_kva, g_m_kr,
                  w_uk, g_m_kn, w_uv):
    D = w_in.shape[0]
    o = [int(v) for v in np.cumsum(IN_SPLITS)]

    def padc(a, width=LANES):
        return jnp.pad(a, ((0, 0), (0, width - a.shape[1])))

    w1 = jnp.concatenate([
        w_in[:, 0:o[3]], padc(w_in[:, o[3]:o[4]]), padc(w_in[:, o[4]:o[5]]), w_in[:, o[5]:o[7]],
        padc(w_in[:, o[7]:o[8]]), w_in[:, o[8]:o[10]]], axis=1).astype(BF16)
    wq = w_uq.reshape(M_Q_RANK, M_HEADS, M_NOPE + M_ROPE)
    wuq = jnp.concatenate(
        [wq[:, :, :M_NOPE].reshape(M_Q_RANK, M_HEADS * M_NOPE)]
        + [padc(wq[:, h, M_NOPE:]) for h in range(M_HEADS)], axis=1).astype(BF16)
    e64 = jnp.asarray(np.kron(np.eye(8, dtype=np.float32), np.ones((64, 64), np.float32)), BF16)
    row = lambda v: v.reshape(1, -1).astype(F32)
    return dict(
        gattn=row(g_attn), w1=w1, bgate=b_gate.astype(F32), e64=e64,
        gaq=row(jnp.tile(g_a_q, A_HEADS)), gak=row(jnp.tile(g_a_k, A_KV_HEADS)), gqa=row(g_m_qa), wuq=wuq,
        gqn=row(jnp.tile(g_m_qn, M_HEADS)), gqr=padc(row(g_m_qr)), gkva=row(g_m_kva), gkr=padc(row(g_m_kr)),
        wuk=w_uk.reshape(M_KV_RANK, M_HEADS * M_NOPE).astype(BF16), gkn=row(jnp.tile(g_m_kn, M_HEADS)),
        wuv=w_uv.reshape(M_KV_RANK, M_HEADS * M_V).astype(BF16))


def _rope_tables(pos):
    half = M_ROPE // 2
    inv = ROPE_BASE ** (-jnp.arange(half, dtype=F32) / half)
    ang = pos.astype(F32)[:, None] * inv
    cos, sin = jnp.cos(ang), jnp.sin(ang)
    n = pos.shape[0]
    ctab = jnp.concatenate([cos, cos, jnp.ones((n, LANES - 2 * half), F32)], axis=1)
    s1tab = jnp.concatenate([-sin, jnp.zeros((n, LANES - half), F32)], axis=1)
    s2tab = jnp.concatenate([jnp.zeros((n, half), F32), sin, jnp.zeros((n, LANES - 2 * half), F32)], axis=1)
    return ctab, s1tab, s2tab


def _proj(x, pos, pw, tm):
    B, S, D = x.shape
    ctab, s1tab, s2tab = _rope_tables(pos)
    tok = lambda w: pl.BlockSpec((1, tm, w), lambda s, b: (b, s, 0))
    tokT = lambda w: pl.BlockSpec((1, w, tm), lambda s, b: (b, 0, s))
    tokC = lambda w: pl.BlockSpec((1, 1, w, tm), lambda s, b: (b, s, 0, 0))
    tab = pl.BlockSpec((tm, LANES), lambda s, b: (s, 0))
    full = lambda a: pl.BlockSpec(a.shape, lambda s, b: (0,) * a.ndim)
    wnames = ["gattn", "w1", "bgate", "e64", "gaq", "gak", "gqa", "wuq", "gqn", "gqr", "gkva", "gkr", "wuk", "gkn",
              "wuv"]
    ws = [pw[n] for n in wnames]
    sds = lambda shape, dt: jax.ShapeDtypeStruct(shape, dt)
    out_shape = [
        sds((B, S, 256), F32), sds((B, S, 256), F32), sds((B, S, IDX_DIM), F32), sds((B, S, M_KV_RANK), F32),
        sds((B, S, M_ROPE), F32),
        sds((B, S, 1024), BF16), sds((B, S, 512), BF16), sds((B, S // tm, 256, tm), BF16), sds((B, S, 1024), BF16),
        sds((B, IDX_HEADS, S), F32), sds((B, S, LANES), BF16), sds((B, S, 512), BF16), sds((B, S, 1024), BF16),
        sds((B, S, 1024), BF16), sds((B, S // tm, 512, tm), BF16), sds((B, S, D), BF16), sds((B, S, D), BF16)]
    out_specs = [tok(256), tok(256), tok(IDX_DIM), tok(M_KV_RANK), tok(M_ROPE),
                 tok(1024), tok(512), tokC(256), tok(1024), tokT(IDX_HEADS), tok(LANES), tok(512), tok(1024),
                 tok(1024), tokC(512), tok(D), tok(D)]
    outs = pl.pallas_call(
        _proj_kernel, out_shape=out_shape, grid=(S // tm, B),
        in_specs=[tok(D), tab, tab, tab] + [full(a) for a in ws], out_specs=out_specs,
        compiler_params=_cparams(("arbitrary", "arbitrary")), name="proj")(x, ctab, s1tab, s2tab, *ws)
    names = ["ak", "av", "ik", "ckv", "kr", "aqp", "akp", "avT", "iqp", "iwT", "ikb", "qn", "qr", "kp", "vmT",
             "ga", "gb"]
    return dict(zip(names, outs))


INT_MIN = -2 ** 31


def _rel_bucket(dist):
    n = jnp.maximum(dist, 0)
    max_exact = REL_BUCKETS // 2
    nf = jnp.maximum(n, 1).astype(F32)
    large = max_exact + (jnp.log(nf / max_exact) / math.log(REL_MAX_DIST / max_exact)
                         * (REL_BUCKETS - max_exact)).astype(I32)
    large = jnp.minimum(large, REL_BUCKETS - 1)
    return jnp.where(n < max_exact, n, large)


def _bias_tiles(rel_bias):
    qq = jnp.arange(Q_BLOCK, dtype=I32)[None, :]
    kk = jnp.arange(Q_BLOCK, dtype=I32)[:, None]
    tiles = [jnp.moveaxis(rel_bias[_rel_bucket(o * Q_BLOCK + qq - kk)], -1, 0) for o in range(3)]
    return jnp.stack(tiles).astype(F32)


def _sortable_key(sc):
    bits = pltpu.bitcast(sc, I32)
    key = jnp.where(bits < 0, bits ^ jnp.int32(0x7FFFFFFF), bits)
    return jnp.where(bits == jnp.int32(INT_MIN), 0, key)


def _colsum8(m):
    return m.reshape(m.shape[0] // 8, 8, LANES).sum(axis=0)


def _select_topk_columns(key_ref, n_chunks, kb, topk, active, idx_bits):
    def count(pred):
        def body(c, acc):
            r0 = pl.multiple_of(c * kb, kb)
            k = key_ref[pl.ds(r0, kb), :]
            return acc + _colsum8(pred(k, r0).astype(I32))
        acc = lax.fori_loop(0, n_chunks, body, jnp.zeros((8, LANES), I32))
        return jnp.sum(acc, axis=0, keepdims=True)

    zero = jnp.zeros((1, LANES), I32)
    cnt0 = count(lambda k, r0: k >= zero)
    t0 = jnp.where(jnp.logical_and(cnt0 >= topk, active), 0, INT_MIN).astype(I32)

    def bit_body(j, t):
        cand = t | jnp.left_shift(jnp.int32(1), 30 - j)
        cnt = count(lambda k, r0: k >= cand)
        return jnp.where(cnt >= topk, cand, t)

    t = lax.fori_loop(0, jnp.where(active, 31, 0), bit_body, t0)
    need = topk - count(lambda k, r0: k > t)

    def tie_body(j, cut):
        cand = cut | jnp.left_shift(jnp.int32(1), idx_bits - 1 - j)

        def pred(k, r0):
            row = r0 + lax.broadcasted_iota(I32, (kb, LANES), 0)
            return jnp.logical_and(k == t, row < cand)
        return jnp.where(count(pred) <= need, cand, cut)

    cut = lax.fori_loop(0, jnp.where(active, idx_bits, 0), tie_body, zero)
    return t, cut


def _softmax_step(s, m_old, l_old):
    m_new = jnp.maximum(m_old, jnp.max(s, axis=0, keepdims=True))
    m_safe = jnp.where(m_new == NEG_INF, 0.0, m_new)
    p = jnp.exp(s - m_safe)
    alpha = jnp.exp(m_old - m_safe)
    l_new = alpha * l_old + jnp.sum(p, axis=0, keepdims=True)
    return p, alpha, m_new, l_new


def _dsa_prompt_kernel(aq_ref, iq_ref, iw_ref, ak_ref, ik_ref, avt_ref, bias_ref, y_ref,
                       sc_ref, key_ref, acc_ref, m_ref, l_ref, *, topk, kb, idx_bits):
    i = pl.program_id(1)
    n_keys = (i + 1) * Q_BLOCK
    n_chunks = (n_keys + kb - 1) // kb
    q_pos = i * Q_BLOCK + lax.broadcasted_iota(I32, (kb, LANES), 1)
    row_iota = lax.broadcasted_iota(I32, (kb, LANES), 0)

    iq = iq_ref[0]
    iq_stack = jnp.concatenate([iq[:, h * LANES:(h + 1) * LANES] for h in range(IDX_HEADS)], axis=0)
    iw = iw_ref[0]

    def score_body(c, carry):
        r0 = pl.multiple_of(c * kb, kb)
        d = _dot_nt(ik_ref[0, pl.ds(r0, kb), :], iq_stack)
        sc = iw[0:1, :] * jnp.maximum(d[:, 0:LANES], 0.0)
        for h in range(1, IDX_HEADS):
            sc = sc + iw[h:h + 1, :] * jnp.maximum(d[:, h * LANES:(h + 1) * LANES], 0.0)
        sc = jnp.where(r0 + row_iota <= q_pos, sc, NEG_INF)
        key_ref[pl.ds(r0, kb), :] = _sortable_key(sc)
        return carry

    lax.fori_loop(0, n_chunks, score_body, 0)
    t, cut = _select_topk_columns(key_ref, n_chunks, kb, topk, n_keys > topk, idx_bits)

    def mask_body(c, carry):
        r0 = pl.multiple_of(c * kb, kb)
        k = key_ref[pl.ds(r0, kb), :]
        row = r0 + row_iota
        sel = jnp.logical_or(k > t, jnp.logical_and(k == t, row < cut))
        sc_ref[pl.ds(r0, kb), :] = jnp.where(jnp.logical_and(sel, row <= q_pos), 0.0, NEG_INF)
        return carry

    lax.fori_loop(0, n_chunks, mask_body, 0)

    aq = aq_ref[0]
    aq_stack = [jnp.concatenate([aq[:, (2 * g) * LANES:(2 * g + 1) * LANES],
                                 aq[:, (2 * g + 1) * LANES:(2 * g + 2) * LANES]], axis=0)
                for g in range(A_KV_HEADS)]
    m_ref[...] = jnp.full(m_ref.shape, NEG_INF, F32)
    l_ref[...] = jnp.zeros(l_ref.shape, F32)
    acc_ref[...] = jnp.zeros(acc_ref.shape, F32)
    n_sub = kb // Q_BLOCK

    def attn_body(c, carry):
        r0 = pl.multiple_of(c * kb, kb)
        addm = sc_ref[pl.ds(r0, kb), :]
        tile_idx = [jnp.clip(i - (c * n_sub + u), 0, 2) for u in range(n_sub)]
        for g in range(A_KV_HEADS):
            s_pair = _dot_nt(ak_ref[0, pl.ds(r0, kb), g * LANES:(g + 1) * LANES], aq_stack[g])
            ps, alphas = [], []
            for hh in range(2):
                h = 2 * g + hh
                bias = jnp.concatenate([bias_ref[tile_idx[u], h] for u in range(n_sub)], axis=0)
                s = s_pair[:, hh * LANES:(hh + 1) * LANES] + bias + addm
                p, alpha, m_new, l_new = _softmax_step(s, m_ref[h:h + 1, :], l_ref[h:h + 1, :])
                m_ref[h:h + 1, :] = m_new
                l_ref[h:h + 1, :] = l_new
                ps.append(p.astype(BF16))
                alphas.append(alpha)
            o_pair = _dot(avt_ref[0, c, g * A_HEAD_DIM:(g + 1) * A_HEAD_DIM, :], jnp.concatenate(ps, axis=1))
            for hh in range(2):
                h = 2 * g + hh
                rows = slice(h * A_HEAD_DIM, (h + 1) * A_HEAD_DIM)
                acc_ref[rows, :] = alphas[hh] * acc_ref[rows, :] + o_pair[:, hh * LANES:(hh + 1) * LANES]
        return carry

    lax.fori_loop(0, n_chunks, attn_body, 0)
    outs = [acc_ref[h * A_HEAD_DIM:(h + 1) * A_HEAD_DIM, :] / l_ref[h:h + 1, :] for h in range(A_HEADS)]
    y_ref[0] = jnp.concatenate(outs, axis=0).T.astype(BF16)


def _dsa_prompt(p, bias_tiles, kb):
    B, S, _ = p["aqp"].shape
    topk = min(TOPK_MAX, S // 4)
    nqb = S // Q_BLOCK
    idx_bits = int(S).bit_length()
    qblk = lambda w: pl.BlockSpec((1, Q_BLOCK, w), lambda b, i: (b, i, 0))
    kern = functools.partial(_dsa_prompt_kernel, topk=topk, kb=kb, idx_bits=idx_bits)
    return pl.pallas_call(
        kern, out_shape=jax.ShapeDtypeStruct((B, S, A_HEADS * A_HEAD_DIM), BF16), grid=(B, nqb),
        in_specs=[qblk(1024), qblk(1024), pl.BlockSpec((1, IDX_HEADS, Q_BLOCK), lambda b, i: (b, 0, i)),
                  pl.BlockSpec((1, S, 512), lambda b, i: (b, 0, 0)),
                  pl.BlockSpec((1, S, LANES), lambda b, i: (b, 0, 0)),
                  pl.BlockSpec((1, S // kb, 256, kb), lambda b, i: (b, 0, 0, 0)),
                  pl.BlockSpec(bias_tiles.shape, lambda b, i: (0, 0, 0, 0))],
        out_specs=qblk(A_HEADS * A_HEAD_DIM),
        scratch_shapes=[pltpu.VMEM((S, LANES), F32), pltpu.VMEM((S, LANES), I32),
                        pltpu.VMEM((A_HEADS * A_HEAD_DIM, LANES), F32), pltpu.VMEM((A_HEADS, LANES), F32),
                        pltpu.VMEM((A_HEADS, LANES), F32)],
        compiler_params=_cparams(("arbitrary", "arbitrary")), name="dsa_prompt",
    )(p["aqp"], p["iqp"], p["iwT"], p["akp"], p["ikb"], p["avT"], bias_tiles)


def _causal_tiles():
    qq = np.arange(Q_BLOCK)[None, :]
    kk = np.arange(Q_BLOCK)[:, None]
    tri = np.where(kk <= qq, 0.0, -np.inf).astype(np.float32)
    return jnp.asarray(np.stack([np.zeros_like(tri), tri, np.full_like(tri, -np.inf)]))


def _mla_prompt_kernel(qn_ref, qr_ref, kp_ref, vmt_ref, mask_ref, y_ref, acc_ref, m_ref, l_ref, *, kb):
    i = pl.program_id(1)
    n_chunks = ((i + 1) * Q_BLOCK + kb - 1) // kb
    n_sub = kb // Q_BLOCK
    lo = lax.broadcasted_iota(I32, (Q_BLOCK, LANES), 1) < M_NOPE
    qn, qr = qn_ref[0], qr_ref[0]
    zero = jnp.zeros((Q_BLOCK, LANES), BF16)
    q_stack = []
    for j in range(M_HEADS // 2):
        c = qn[:, j * LANES:(j + 1) * LANES]
        top = jnp.concatenate([jnp.where(lo, c, zero), qr[:, (2 * j) * LANES:(2 * j + 1) * LANES]], axis=1)
        bot = jnp.concatenate([jnp.where(lo, zero, c), qr[:, (2 * j + 1) * LANES:(2 * j + 2) * LANES]], axis=1)
        q_stack.append(jnp.concatenate([top, bot], axis=0))
    m_ref[...] = jnp.full(m_ref.shape, NEG_INF, F32)
    l_ref[...] = jnp.zeros(l_ref.shape, F32)
    acc_ref[...] = jnp.zeros(acc_ref.shape, F32)

    def attn_body(c, carry):
        r0 = pl.multiple_of(c * kb, kb)
        addm = jnp.concatenate([mask_ref[jnp.clip(c * n_sub + u - i + 1, 0, 2)] for u in range(n_sub)], axis=0)
        for j in range(M_HEADS // 2):
            s_pair = _dot_nt(kp_ref[0, pl.ds(r0, kb), j * 2 * LANES:(j + 1) * 2 * LANES], q_stack[j])
            for hh in range(2):
                h = 2 * j + hh
                s = s_pair[:, hh * LANES:(hh + 1) * LANES] + addm
                p, alpha, m_new, l_new = _softmax_step(s, m_ref[h:h + 1, :], l_ref[h:h + 1, :])
                m_ref[h:h + 1, :] = m_new
                l_ref[h:h + 1, :] = l_new
                rows = slice(h * M_V, (h + 1) * M_V)
                o = _dot(vmt_ref[0, c, rows, :], p.astype(BF16))
                acc_ref[rows, :] = alpha * acc_ref[rows, :] + o
        return carry

    lax.fori_loop(0, n_chunks, attn_body, 0)
    outs = [acc_ref[h * M_V:(h + 1) * M_V, :] / l_ref[h:h + 1, :] for h in range(M_HEADS)]
    y_ref[0] = jnp.concatenate(outs, axis=0).T.astype(BF16)


def _mla_prompt(p, kb):
    B, S, _ = p["qn"].shape
    qblk = lambda w: pl.BlockSpec((1, Q_BLOCK, w), lambda b, i: (b, i, 0))
    masks = _causal_tiles()
    return pl.pallas_call(
        functools.partial(_mla_prompt_kernel, kb=kb),
        out_shape=jax.ShapeDtypeStruct((B, S, M_HEADS * M_V), BF16), grid=(B, S // Q_BLOCK),
        in_specs=[qblk(512), qblk(1024), pl.BlockSpec((1, S, 1024), lambda b, i: (b, 0, 0)),
                  pl.BlockSpec((1, S // kb, 512, kb), lambda b, i: (b, 0, 0, 0)),
                  pl.BlockSpec(masks.shape, lambda b, i: (0, 0, 0))],
        out_specs=qblk(M_HEADS * M_V),
        scratch_shapes=[pltpu.VMEM((M_HEADS * M_V, LANES), F32), pltpu.VMEM((M_HEADS, LANES), F32),
                        pltpu.VMEM((M_HEADS, LANES), F32)],
        compiler_params=_cparams(("arbitrary", "arbitrary")), name="mla_prompt",
    )(p["qn"], p["qr"], p["kp"], p["vmT"], masks)


PAGES_PER_STEP = 8


def _page_specs(shape_tail, n_pages):
    def spec(u):
        def index_map(b, j, pt):
            return (0, pt[b * n_pages + j * PAGES_PER_STEP + u]) + (0,) * len(shape_tail)
        return pl.BlockSpec((1, 1) + tuple(shape_tail), index_map)
    return [spec(u) for u in range(PAGES_PER_STEP)]


def _lane_column(mat, b):
    lane = lax.broadcasted_iota(I32, mat.shape, 1)
    return jnp.sum(jnp.where(lane == b, mat, 0.0), axis=1, keepdims=True)


def _idx_sample_kernel(pt_ref, iq_ref, iwt_ref, iknew_ref, *rest):
    del pt_ref
    pages = rest[:PAGES_PER_STEP]
    sc_o, scnew_o = rest[PAGES_PER_STEP:]
    b, j = pl.program_id(0), pl.program_id(1)
    iq = iq_ref[0][:, 0:IDX_DIM]
    wcol = _lane_column(iwt_ref[...], b)

    def page_scores(keys):
        d = _dot_nt(iq, keys.astype(BF16))
        return jnp.sum(wcol * jnp.maximum(d, 0.0), axis=0, keepdims=True)

    for u in range(PAGES_PER_STEP):
        sc_o[0, u:u + 1, :] = page_scores(pages[u][0, 0])

    @pl.when(j == 0)
    def _():
        sc = page_scores(iknew_ref[0])
        lane = lax.broadcasted_iota(I32, (1, LANES), 1)
        scnew_o[0] = jnp.where(lane == 0, sc, NEG_INF)


def _idx_sample(pt_flat, iq3, iwT, iknew_page, cache_idx, n_pages):
    DB = iq3.shape[0]
    grid_spec = pltpu.PrefetchScalarGridSpec(
        num_scalar_prefetch=1, grid=(DB, n_pages // PAGES_PER_STEP),
        in_specs=[pl.BlockSpec((1, IDX_HEADS, LANES), lambda b, j, pt: (b, 0, 0)),
                  pl.BlockSpec(iwT.shape, lambda b, j, pt: (0, 0)),
                  pl.BlockSpec((1, PAGE_SIZE, IDX_DIM), lambda b, j, pt: (b, 0, 0))]
        + _page_specs((PAGE_SIZE, IDX_DIM), n_pages),
        out_specs=[pl.BlockSpec((1, PAGES_PER_STEP, LANES), lambda b, j, pt: (b, j, 0)),
                   pl.BlockSpec((1, 1, LANES), lambda b, j, pt: (b, 0, 0))])
    return pl.pallas_call(
        _idx_sample_kernel,
        out_shape=[jax.ShapeDtypeStruct((DB, n_pages, LANES), F32), jax.ShapeDtypeStruct((DB, 1, LANES), F32)],
        grid_spec=grid_spec, compiler_params=_cparams(("arbitrary", "arbitrary")), name="idx_sample",
    )(pt_flat, iq3, iwT, iknew_page, *([cache_idx] * PAGES_PER_STEP))


def _sel_sample_kernel(sc_ref, sel_ref, key_ref, *, topk, kb, idx_bits):
    n_rows = sc_ref.shape[0]
    n_chunks = n_rows // kb
    row_iota = lax.broadcasted_iota(I32, (kb, LANES), 0)

    def key_body(c, carry):
        r0 = pl.multiple_of(c * kb, kb)
        key_ref[pl.ds(r0, kb), :] = _sortable_key(sc_ref[pl.ds(r0, kb), :])
        return carry

    lax.fori_loop(0, n_chunks, key_body, 0)
    t, cut = _select_topk_columns(key_ref, n_chunks, kb, topk, jnp.bool_(True), idx_bits)

    def mask_body(c, carry):
        r0 = pl.multiple_of(c * kb, kb)
        k = key_ref[pl.ds(r0, kb), :]
        sel = jnp.logical_or(k > t, jnp.logical_and(k == t, r0 + row_iota < cut))
        sel_ref[pl.ds(r0, kb), :] = sel.astype(F32)
        return carry

    lax.fori_loop(0, n_chunks, mask_body, 0)


def _sel_sample(scores_t, topk):
    L, DB = scores_t.shape
    kern = functools.partial(_sel_sample_kernel, topk=topk, kb=PAGE_SIZE, idx_bits=int(L).bit_length())
    return pl.pallas_call(
        kern, out_shape=jax.ShapeDtypeStruct((L, DB), F32), grid=(1,),
        in_specs=[pl.BlockSpec((L, DB), lambda i: (0, 0))], out_specs=pl.BlockSpec((L, DB), lambda i: (0, 0)),
        scratch_shapes=[pltpu.VMEM((L, DB), I32)],
        compiler_params=_cparams(("arbitrary",)), name="sel_sample")(scores_t)


COMPACT_PAGES = 256


def _compact_kernel(pt_ref, sel_ref, info_o, *, n_pages, topk):
    b = pl.program_id(0)
    sel = sel_ref[0]
    selb = sel.astype(BF16)
    r_i = lax.broadcasted_iota(I32, (LANES, LANES), 0)
    c_i = lax.broadcasted_iota(I32, (LANES, LANES), 1)
    upper = (r_i < c_i).astype(BF16)
    within = _dot(selb, upper)
    counts = _dot(selb, jnp.ones((LANES, LANES), BF16))
    pr = lax.broadcasted_iota(I32, (COMPACT_PAGES, COMPACT_PAGES), 0)
    pc = lax.broadcasted_iota(I32, (COMPACT_PAGES, COMPACT_PAGES), 1)
    before = _dot((pc < pr).astype(BF16), counts.astype(BF16))
    rank = (before + within).astype(I32)
    rr = lax.broadcasted_iota(I32, (topk, LANES), 0)
    lane = lax.broadcasted_iota(I32, (PAGE_SIZE, LANES), 1)
    slot = lax.broadcasted_iota(I32, (PAGE_SIZE, LANES), 0)
    info = jnp.zeros((topk, LANES), F32)
    n_groups = -(-(n_pages + 1) // 8)
    for grp in range(n_groups):
        hits, vals = [], []
        for u in range(8):
            p = grp * 8 + u
            hit = jnp.logical_and(rr == rank[p:p + 1, :], sel[p:p + 1, :] > 0.0)
            hits.append(hit.astype(BF16))
            phys = pt_ref[b * n_pages + min(p, n_pages - 1)]
            v = jnp.where(lane == 0, slot, jnp.where(lane == 1, p, jnp.where(lane == 2, phys % LANES,
                          jnp.where(lane == 3, phys // LANES, 0))))
            vals.append(v.astype(F32).astype(BF16))
        info = info + _dot(jnp.concatenate(hits, axis=1), jnp.concatenate(vals, axis=0))
    info_o[0] = info


def _compact(pt_flat, sel3, n_pages, topk):
    DB = sel3.shape[0]
    grid_spec = pltpu.PrefetchScalarGridSpec(
        num_scalar_prefetch=1, grid=(DB,),
        in_specs=[pl.BlockSpec((1, COMPACT_PAGES, LANES), lambda b, pt: (b, 0, 0))],
        out_specs=pl.BlockSpec((1, topk, LANES), lambda b, pt: (b, 0, 0)))
    return pl.pallas_call(
        functools.partial(_compact_kernel, n_pages=n_pages, topk=topk),
        out_shape=jax.ShapeDtypeStruct((DB, topk, LANES), F32), grid_spec=grid_spec,
        compiler_params=_cparams(("arbitrary",)), name="compact_sample")(pt_flat, sel3)


def _dsa_sample_kernel(gidx_ref, th_ref, aq_ref, dist_ref, isnew_ref, knew_ref, vnew_ref, rbt_ref, ck_ref, cv_ref,
                       y_o, ksel_ref, vsel_ref, sem, *, topk):
    b = pl.program_id(0)

    def copies(r):
        row = gidx_ref[b * topk + r]
        return (pltpu.make_async_copy(ck_ref.at[pl.ds(row, 1), :], ksel_ref.at[pl.ds(r, 1), :], sem),
                pltpu.make_async_copy(cv_ref.at[pl.ds(row, 1), :], vsel_ref.at[pl.ds(r, 1), :], sem))

    def start(r, c):
        for cp in copies(r):
            cp.start()
        return c

    def wait(r, c):
        for cp in copies(r):
            cp.wait()
        return c

    lax.fori_loop(0, topk, start, 0)
    lax.fori_loop(0, topk, wait, 0)
    isnew = isnew_ref[0] > 0.0
    ksel = jnp.where(isnew, knew_ref[0], ksel_ref[...]).astype(BF16)
    vsel = jnp.where(isnew, vnew_ref[0], vsel_ref[...]).astype(BF16)
    aq = aq_ref[0]
    aq_hi = pltpu.roll(aq.astype(F32), 64, 1).astype(BF16)
    kvh = lax.broadcasted_iota(I32, (A_HEADS, LANES), 0) // (A_HEADS // A_KV_HEADS)
    zero = jnp.zeros((A_HEADS, LANES), BF16)
    q_bd = jnp.concatenate(
        [jnp.where(kvh == 0, aq, zero) + jnp.where(kvh == 1, aq_hi, zero),
         jnp.where(kvh == 2, aq, zero) + jnp.where(kvh == 3, aq_hi, zero)], axis=1)
    s = _dot_nt(q_bd, ksel)
    dist = dist_ref[0]
    bias = jnp.zeros((A_HEADS, topk), F32) + rbt_ref[:, 0:1]
    for bk in range(1, REL_BUCKETS):
        bias = jnp.where(dist >= th_ref[bk], rbt_ref[:, bk:bk + 1], bias)
    s = s + bias
    m = jnp.max(s, axis=1, keepdims=True)
    p = jnp.exp(s - m)
    p = p / jnp.sum(p, axis=1, keepdims=True)
    o = _dot(p.astype(BF16), vsel)
    lo, hi = o[:, 0:LANES], o[:, LANES:2 * LANES]
    y = jnp.where(kvh == 0, lo, jnp.where(kvh == 1, pltpu.roll(lo, 64, 1),
                  jnp.where(kvh == 2, hi, pltpu.roll(hi, 64, 1))))
    y_o[0] = y


def _dsa_sample(gidx_flat, th, aq3, dist, isnew, knew, vnew, rbt, ck2, cv2, topk):
    DB = aq3.shape[0]
    grid_spec = pltpu.PrefetchScalarGridSpec(
        num_scalar_prefetch=2, grid=(DB,),
        in_specs=[pl.BlockSpec((1, A_HEADS, LANES), lambda b, g, t: (b, 0, 0)),
                  pl.BlockSpec((1, 1, topk), lambda b, g, t: (b, 0, 0)),
                  pl.BlockSpec((1, topk, 1), lambda b, g, t: (b, 0, 0)),
                  pl.BlockSpec((1, 1, 256), lambda b, g, t: (b, 0, 0)),
                  pl.BlockSpec((1, 1, 256), lambda b, g, t: (b, 0, 0)),
                  pl.BlockSpec(rbt.shape, lambda b, g, t: (0, 0)),
                  pl.BlockSpec(memory_space=pl.ANY), pl.BlockSpec(memory_space=pl.ANY)],
        out_specs=pl.BlockSpec((1, A_HEADS, LANES), lambda b, g, t: (b, 0, 0)),
        scratch_shapes=[pltpu.VMEM((topk, 256), F32), pltpu.VMEM((topk, 256), F32), pltpu.SemaphoreType.DMA(())])
    return pl.pallas_call(
        functools.partial(_dsa_sample_kernel, topk=topk),
        out_shape=jax.ShapeDtypeStruct((DB, A_HEADS, LANES), F32), grid_spec=grid_spec,
        compiler_params=_cparams(("arbitrary",)), name="dsa_sample",
    )(gidx_flat, th, aq3, dist, isnew, knew, vnew, rbt, ck2, cv2)


def _mla_sample_kernel(pt_ref, qgt_ref, qr_ref, latnew_ref, krnew_ref, wukt_ref, wuv_ref, *rest,
                       n_steps):
    del pt_ref
    lat_pages = rest[:PAGES_PER_STEP]
    kr_pages = rest[PAGES_PER_STEP:2 * PAGES_PER_STEP]
    y_o, qcol_ref, acc_ref, m_ref, l_ref = rest[2 * PAGES_PER_STEP:]
    b, j = pl.program_id(0), pl.program_id(1)
    qr = qr_ref[0][:, 0:M_ROPE]

    def attend(lat, kr, valid):
        nk = lat.shape[0]
        latb = lat.astype(BF16)
        knt = _dot_nt(wukt_ref[...], latb)
        qcol = qcol_ref[...]
        if nk > LANES:
            qcol = jnp.concatenate([qcol] * (nk // LANES), axis=1)
        rows = []
        for h in range(M_HEADS):
            kh = knt[h * M_NOPE:(h + 1) * M_NOPE, :]
            ss = jnp.sum(kh * kh, axis=0, keepdims=True) * (1.0 / M_NOPE)
            raw = jnp.sum(kh * qcol[h * M_NOPE:(h + 1) * M_NOPE, :], axis=0, keepdims=True)
            rows.append(raw * lax.rsqrt(ss + EPS))
        s = jnp.concatenate(rows, axis=0) + _dot_nt(qr, kr.astype(BF16))
        if valid is not None:
            s = jnp.where(valid, s, NEG_INF)
        m_old = m_ref[...]
        m_new = jnp.maximum(m_old, jnp.max(s, axis=1, keepdims=True))
        m_safe = jnp.where(m_new == NEG_INF, 0.0, m_new)
        p = jnp.exp(s - m_safe)
        alpha = jnp.exp(m_old - m_safe)
        l_ref[...] = alpha * l_ref[...] + jnp.sum(p, axis=1, keepdims=True)
        m_ref[...] = m_new
        acc_ref[...] = alpha * acc_ref[...] + _dot(p.astype(BF16), latb)

    @pl.when(j == 0)
    def _():
        qcol_ref[...] = jnp.zeros(qcol_ref.shape, F32) + _lane_column(qgt_ref[...], b)
        m_ref[...] = jnp.full(m_ref.shape, NEG_INF, F32)
        l_ref[...] = jnp.zeros(l_ref.shape, F32)
        acc_ref[...] = jnp.zeros(acc_ref.shape, F32)
        lane = lax.broadcasted_iota(I32, (M_HEADS, LANES), 1)
        attend(latnew_ref[0], krnew_ref[0], lane == 0)

    for u in range(0, PAGES_PER_STEP, 2):
        lat = jnp.concatenate([lat_pages[u][0, 0], lat_pages[u + 1][0, 0]], axis=0)
        kr = jnp.concatenate([kr_pages[u][0, 0], kr_pages[u + 1][0, 0]], axis=0)
        attend(lat, kr, None)

    @pl.when(j == n_steps - 1)
    def _():
        lat_out = (acc_ref[...] / l_ref[...]).astype(BF16)
        full = _dot(lat_out, wuv_ref[...])
        head = lax.broadcasted_iota(I32, full.shape, 1) // M_V
        row = lax.broadcasted_iota(I32, full.shape, 0)
        y_o[0] = jnp.sum(jnp.where(head == row, full, 0.0), axis=0, keepdims=True)


def _mla_sample(pt_flat, qgt, qr3, latnew, krnew, wukt, wuv, cache_lat, cache_kr, n_pages):
    DB = qr3.shape[0]
    n_steps = n_pages // PAGES_PER_STEP
    grid_spec = pltpu.PrefetchScalarGridSpec(
        num_scalar_prefetch=1, grid=(DB, n_steps),
        in_specs=[pl.BlockSpec(qgt.shape, lambda b, j, pt: (0, 0)),
                  pl.BlockSpec((1, M_HEADS, LANES), lambda b, j, pt: (b, 0, 0)),
                  pl.BlockSpec((1, PAGE_SIZE, M_KV_RANK), lambda b, j, pt: (b, 0, 0)),
                  pl.BlockSpec((1, PAGE_SIZE, M_ROPE), lambda b, j, pt: (b, 0, 0)),
                  pl.BlockSpec(wukt.shape, lambda b, j, pt: (0, 0)),
                  pl.BlockSpec(wuv.shape, lambda b, j, pt: (0, 0))]
        + _page_specs((PAGE_SIZE, M_KV_RANK), n_pages) + _page_specs((PAGE_SIZE, M_ROPE), n_pages),
        out_specs=pl.BlockSpec((1, 1, M_HEADS * M_V), lambda b, j, pt: (b, 0, 0)),
        scratch_shapes=[pltpu.VMEM((M_HEADS * M_NOPE, LANES), F32), pltpu.VMEM((M_HEADS, M_KV_RANK), F32),
                        pltpu.VMEM((M_HEADS, 1), F32), pltpu.VMEM((M_HEADS, 1), F32)])
    return pl.pallas_call(
        functools.partial(_mla_sample_kernel, n_steps=n_steps),
        out_shape=jax.ShapeDtypeStruct((DB, 1, M_HEADS * M_V), F32), grid_spec=grid_spec,
        compiler_params=_cparams(("arbitrary", "arbitrary")), name="mla_sample",
    )(pt_flat, qgt, qr3, latnew, krnew, wukt, wuv, *([cache_lat] * PAGES_PER_STEP), *([cache_kr] * PAGES_PER_STEP))


ROUTER_ROWS = 40


def _merge_kernel(x_ref, ya_ref, yb_ref, ga_ref, gb_ref, woa_ref, wob_ref, wout_ref, gffn_ref,
                  wrh_ref, wrl_ref, br_ref, tri_ref,
                  h_o, hn_o, eid_o, rank_o, wcol_o, cnt_o, cnt_ref):
    tm = x_ref.shape[0]

    @pl.when(pl.program_id(0) == 0)
    def _():
        cnt_ref[...] = jnp.zeros(cnt_ref.shape, F32)

    merged = (ga_ref[...].astype(F32) * _dot(ya_ref[...], woa_ref[...])
              + gb_ref[...].astype(F32) * _dot(yb_ref[...], wob_ref[...]))
    h = x_ref[...] + _dot(merged.astype(BF16), wout_ref[...])
    h_o[...] = h
    hn = h * lax.rsqrt(jnp.mean(h * h, axis=-1, keepdims=True) + EPS) * gffn_ref[...]
    hn_o[...] = hn
    hh = hn.astype(BF16)
    hl = (hn - hh.astype(F32)).astype(BF16)
    logit = (_dot_nt(wrh_ref[...], hh) + _dot_nt(wrh_ref[...], hl) + _dot_nt(wrl_ref[...], hh)) + br_ref[...]
    row8 = lax.broadcasted_iota(I32, (8, tm), 0)
    gl = logit[0:8, :]
    gmax = jnp.max(gl, axis=0, keepdims=True)
    gtop = jnp.min(jnp.where(gl == gmax, row8, 8), axis=0, keepdims=True)
    pg = 1.0 / jnp.sum(jnp.exp(gl - gmax), axis=0, keepdims=True)
    elg = logit[8:16, :]
    for g in range(1, N_GROUPS):
        elg = jnp.where(gtop == g, logit[8 + 8 * g:16 + 8 * g, :], elg)
    m1 = jnp.max(elg, axis=0, keepdims=True)
    i1 = jnp.min(jnp.where(elg == m1, row8, 8), axis=0, keepdims=True)
    elg2 = jnp.where(row8 == i1, NEG_INF, elg)
    m2 = jnp.max(elg2, axis=0, keepdims=True)
    i2 = jnp.min(jnp.where(elg2 == m2, row8, 8), axis=0, keepdims=True)
    e2 = jnp.exp(m2 - m1)
    den = 1.0 + e2
    w0 = (1.0 / den) * pg
    w1 = (e2 / den) * pg
    eid0 = gtop * EXPERTS_PER_GROUP + i1
    eid1 = gtop * EXPERTS_PER_GROUP + i2
    erow = lax.broadcasted_iota(I32, (N_EXPERTS, tm), 0)
    oh0 = erow == eid0
    oh1 = erow == eid1
    ohs = oh0.astype(F32) + oh1.astype(F32)
    base = cnt_ref[...] + _dot(ohs.astype(BF16), tri_ref[...])
    rank0 = jnp.sum(jnp.where(oh0, base, 0.0), axis=0, keepdims=True)
    rank1 = jnp.sum(jnp.where(oh1, base, 0.0), axis=0, keepdims=True)
    cnt_ref[...] = cnt_ref[...] + jnp.sum(ohs, axis=1, keepdims=True)
    cnt_o[...] = cnt_ref[...]
    eid_o[...] = jnp.concatenate([eid0, eid1], axis=0)
    rank_o[...] = jnp.concatenate([rank0, rank1], axis=0).astype(I32)
    wcol_o[...] = jnp.concatenate([w0, w1, jnp.zeros((LANES - 2, tm), F32)], axis=0).T


def _merge(x, ya, yb, ga, gb, mw, tm):
    N, D = x.shape
    tok = lambda w: pl.BlockSpec((tm, w), lambda i: (i, 0))
    rows = lambda r: pl.BlockSpec((r, tm), lambda i: (0, i))
    full = lambda a: pl.BlockSpec(a.shape, lambda i: (0,) * a.ndim)
    tri = jnp.asarray(np.triu(np.ones((tm, tm), np.float32), 1), BF16)
    ws = [mw["woa"], mw["wob"], mw["wout"], mw["gffn"], mw["wrh"], mw["wrl"], mw["br"], tri]
    sds = jax.ShapeDtypeStruct
    return pl.pallas_call(
        _merge_kernel,
        out_shape=[sds((N, D), F32), sds((N, D), F32), sds((2, N), I32), sds((2, N), I32), sds((N, LANES), F32),
                   sds((N_EXPERTS, 1), F32)],
        grid=(N // tm,),
        in_specs=[tok(D), tok(512), tok(512), tok(D), tok(D)] + [full(a) for a in ws],
        out_specs=[tok(D), tok(D), rows(2), rows(2), tok(LANES), pl.BlockSpec((N_EXPERTS, 1), lambda i: (0, 0))],
        scratch_shapes=[pltpu.VMEM((N_EXPERTS, 1), F32)],
        compiler_params=_cparams(("arbitrary",)), name="merge")(x, ya, yb, ga, gb, *ws)


def _merge_weights(w_oa, w_ob, w_out, g_ffn, w_group, b_group, w_expert, b_expert):
    D = w_out.shape[0]
    wr = jnp.concatenate([w_group.T, jnp.zeros((8 - N_GROUPS, D), F32), w_expert.T], axis=0)
    wrh = wr.astype(BF16)
    wrl = (wr - wrh.astype(F32)).astype(BF16)
    br = jnp.concatenate([b_group, jnp.full((8 - N_GROUPS,), NEG_INF, F32), b_expert]).reshape(ROUTER_ROWS, 1)
    return dict(woa=w_oa.astype(BF16), wob=w_ob.astype(BF16), wout=w_out.astype(BF16),
                gffn=g_ffn.reshape(1, D).astype(F32), wrh=wrh, wrl=wrl, br=br.astype(F32))


MOE_ROWS = 256


def _dispatch_kernel(dest_ref, hn_ref, xs_in_ref, xs_ref, sem):
    del xs_in_ref
    tm = hn_ref.shape[0]

    def row_copy(t, k):
        return pltpu.make_async_copy(hn_ref.at[pl.ds(t, 1), :], xs_ref.at[pl.ds(dest_ref[k, t], 1), :], sem)

    def start(t, c):
        row_copy(t, 0).start()
        row_copy(t, 1).start()
        return c

    def wait(t, c):
        row_copy(t, 0).wait()
        row_copy(t, 1).wait()
        return c

    lax.fori_loop(0, tm, start, 0)
    lax.fori_loop(0, tm, wait, 0)


def _dispatch(dest, hn, n_rows, tm):
    N, D = hn.shape
    return pl.pallas_call(
        _dispatch_kernel, out_shape=jax.ShapeDtypeStruct((n_rows, D), F32), grid=(N // tm,),
        in_specs=[pl.BlockSpec((2, tm), lambda i: (0, i), memory_space=pltpu.SMEM),
                  pl.BlockSpec((tm, D), lambda i: (i, 0)), pl.BlockSpec(memory_space=pl.ANY)],
        out_specs=pl.BlockSpec(memory_space=pl.ANY),
        scratch_shapes=[pltpu.SemaphoreType.DMA(())],
        input_output_aliases={2: 0},
        compiler_params=_cparams(("arbitrary",)), name="moe_dispatch")(dest, hn, jnp.zeros((n_rows, D), F32))


def _expert_kernel(blk_e_ref, xs_ref, w1_ref, w3_ref, w2_ref, ys_ref):
    del blk_e_ref
    xb = xs_ref[...].astype(BF16)
    hid = jax.nn.silu(_dot(xb, w1_ref[0])) * _dot(xb, w3_ref[0])
    ys_ref[...] = _dot(hid.astype(BF16), w2_ref[0])


def _experts(blk_e, xs, w1, w3, w2):
    P, D = xs.shape
    n_blocks = P // MOE_ROWS
    grid_spec = pltpu.PrefetchScalarGridSpec(
        num_scalar_prefetch=1, grid=(n_blocks,),
        in_specs=[pl.BlockSpec((MOE_ROWS, D), lambda i, e: (i, 0)),
                  pl.BlockSpec((1, D, D_EXPERT), lambda i, e: (e[i], 0, 0)),
                  pl.BlockSpec((1, D, D_EXPERT), lambda i, e: (e[i], 0, 0)),
                  pl.BlockSpec((1, D_EXPERT, D), lambda i, e: (e[i], 0, 0))],
        out_specs=pl.BlockSpec((MOE_ROWS, D), lambda i, e: (i, 0)))
    return pl.pallas_call(
        _expert_kernel, out_shape=jax.ShapeDtypeStruct((P, D), F32), grid_spec=grid_spec,
        compiler_params=_cparams(("arbitrary",)), name="moe_experts")(blk_e, xs, w1, w3, w2)


def _combine_kernel(dest_ref, h_ref, wcol_ref, ys_ref, out_ref, g_ref, sem):
    tm = h_ref.shape[0]

    def row_copy(t, k):
        return pltpu.make_async_copy(ys_ref.at[pl.ds(dest_ref[k, t], 1), :], g_ref.at[k, pl.ds(t, 1), :], sem)

    def start(t, c):
        row_copy(t, 0).start()
        row_copy(t, 1).start()
        return c

    def wait(t, c):
        row_copy(t, 0).wait()
        row_copy(t, 1).wait()
        return c

    lax.fori_loop(0, tm, start, 0)
    lax.fori_loop(0, tm, wait, 0)
    w = wcol_ref[...]
    out_ref[...] = h_ref[...] + (g_ref[0] * w[:, 0:1] + g_ref[1] * w[:, 1:2])


def _combine(dest, h, wcol, ys, tm):
    N, D = h.shape
    return pl.pallas_call(
        _combine_kernel, out_shape=jax.ShapeDtypeStruct((N, D), F32), grid=(N // tm,),
        in_specs=[pl.BlockSpec((2, tm), lambda i: (0, i), memory_space=pltpu.SMEM),
                  pl.BlockSpec((tm, D), lambda i: (i, 0)), pl.BlockSpec((tm, LANES), lambda i: (i, 0)),
                  pl.BlockSpec(memory_space=pl.ANY)],
        out_specs=pl.BlockSpec((tm, D), lambda i: (i, 0)),
        scratch_shapes=[pltpu.VMEM((2, tm, D), F32), pltpu.SemaphoreType.DMA(())],
        compiler_params=_cparams(("arbitrary",)), name="moe_combine")(dest, h, wcol, ys)


def _merge_and_ffn(x, ya, yb, ga, gb, mw, ew, tm):
    N, D = x.shape
    h, hn, eid, rank, wcol, counts = _merge(x, ya, yb, ga, gb, mw, tm)
    counts = counts[:, 0].astype(I32)
    padded = (counts + MOE_ROWS - 1) // MOE_ROWS * MOE_ROWS
    pad_end = jnp.cumsum(padded)
    pad_start = pad_end - padded
    dest = pad_start[eid] + rank
    n_blocks = -(-(2 * N) // MOE_ROWS) + N_EXPERTS
    blk_e = jnp.minimum(jnp.searchsorted(pad_end, jnp.arange(n_blocks, dtype=I32) * MOE_ROWS, side="right"),
                        N_EXPERTS - 1).astype(I32)
    xs = _dispatch(dest, hn, n_blocks * MOE_ROWS, tm)
    ys = _experts(blk_e, xs, ew["w1"], ew["w3"], ew["w2"])
    return _combine(dest, h, wcol, ys, tm)


PROJ_TILE = 256


def kernel(x_prompt, x_sample, cache_a_k, cache_a_v, cache_idx_k, cache_mla_latent, cache_mla_krope, page_table,
           rel_bias, g_attn, w_in, b_gate, g_a_q, g_a_k, g_m_qa, w_uq, g_m_qn, g_m_qr, g_m_kva, g_m_kr, w_uk, g_m_kn,
           w_uv, w_oa, w_ob, w_out, g_ffn, w_group, b_group, w_expert, b_expert, w_e1, w_e3, w_e2):
    B, S, D = x_prompt.shape
    l = 0
    pw = _proj_weights(g_attn[l], w_in[l], b_gate[l], g_a_q[l], g_a_k[l], g_m_qa[l], w_uq[l], g_m_qn[l], g_m_qr[l],
                       g_m_kva[l], g_m_kr[l], w_uk[l], g_m_kn[l], w_uv[l])
    pp = _proj(x_prompt, jnp.arange(S, dtype=I32), pw, PROJ_TILE)
    ya_p = _dsa_prompt(pp, _bias_tiles(rel_bias), PROJ_TILE)
    yb_p = _mla_prompt(pp, PROJ_TILE)
    mw = _merge_weights(w_oa[l], w_ob[l], w_out[l], g_ffn[l], w_group[l], b_group[l], w_expert[l], b_expert[l])
    ew = dict(w1=w_e1[l].astype(BF16), w3=w_e3[l].astype(BF16), w2=w_e2[l].astype(BF16))
    flat = lambda a: a.reshape(B * S, a.shape[-1])
    y_p = _merge_and_ffn(flat(x_prompt), flat(ya_p), flat(yb_p), flat(pp["ga"]), flat(pp["gb"]), mw, ew, PROJ_TILE)

    DB = x_sample.shape[0]
    n_pages = page_table.shape[1]
    past = n_pages * PAGE_SIZE
    n_pool = cache_a_k.shape[1]
    ps = _proj(x_sample.reshape(1, DB, D), jnp.full((DB,), past, I32), pw, DB)
    ps = {k: v[0] for k, v in ps.items()}
    pt_flat = page_table.reshape(-1).astype(I32)
    heads3 = lambda a: a.reshape(DB, 8, LANES)
    first_row = lambda a: jnp.pad(a[:, None, :], ((0, 0), (0, PAGE_SIZE - 1), (0, 0)))
    sc, sc_new = _idx_sample(pt_flat, heads3(ps["iqp"]), ps["iwT"], first_row(ps["ik"]), cache_idx_k[l:l + 1], n_pages)
    scores = jnp.concatenate([sc.reshape(DB, past), sc_new.reshape(DB, LANES)], axis=1)
    topk = min(TOPK_MAX, (past + 1) // 4)
    sel = _sel_sample(scores.T, topk).T.reshape(DB, n_pages + 1, LANES)
    sel = jnp.pad(sel, ((0, 0), (0, COMPACT_PAGES - n_pages - 1), (0, 0)))
    info = _compact(pt_flat, sel, n_pages, topk).astype(I32)
    slot, page, phys = info[..., 0], info[..., 1], info[..., 3] * LANES + info[..., 2]
    is_new = page >= n_pages
    gidx = jnp.where(is_new, 0, phys * PAGE_SIZE + slot)
    dist = past - (page * PAGE_SIZE + slot)
    buckets = _rel_bucket(jnp.arange(REL_MAX_DIST + 2, dtype=I32))
    th = jnp.sum(buckets[None, :] < jnp.arange(REL_BUCKETS, dtype=I32)[:, None], axis=1).astype(I32)
    ya_s = _dsa_sample(gidx.reshape(-1), th, heads3(ps["aqp"]), dist.reshape(DB, 1, topk),
                       is_new.astype(F32).reshape(DB, topk, 1), ps["ak"].reshape(DB, 1, 256),
                       ps["av"].reshape(DB, 1, 256), rel_bias.T.astype(F32),
                       cache_a_k[l].reshape(n_pool * PAGE_SIZE, 256), cache_a_v[l].reshape(n_pool * PAGE_SIZE, 256), topk)
    ya_s = ya_s[:, :, 0:A_HEAD_DIM].reshape(DB, A_HEADS * A_HEAD_DIM)
    qgt = (ps["qn"].astype(F32) * jnp.tile(g_m_kn[l], M_HEADS)[None, :]).T
    wukt = w_uk[l].reshape(M_KV_RANK, M_HEADS * M_NOPE).T.astype(BF16)
    yb_s = _mla_sample(pt_flat, qgt, heads3(ps["qr"]), first_row(ps["ckv"]), first_row(ps["kr"]), wukt, pw["wuv"],
                       cache_mla_latent[l:l + 1], cache_mla_krope[l:l + 1], n_pages).reshape(DB, M_HEADS * M_V)
    y_s = _merge_and_ffn(x_sample.reshape(DB, D), ya_s.astype(BF16), yb_s.astype(BF16), ps["ga"], ps["gb"], mw, ew, DB)

    kv = lambda a, n: a.reshape(1, n, -1, A_KV_HEADS, A_HEAD_DIM)
    return (y_p.reshape(B, S, D), y_s.reshape(DB, 1, D),
            kv(pp["ak"], B), kv(pp["av"], B), pp["ik"][None], pp["ckv"][None], pp["kr"][None],
            kv(ps["ak"], DB), kv(ps["av"], DB), ps["ik"].reshape(1, DB, 1, IDX_DIM),
            ps["ckv"].reshape(1, DB, 1, M_KV_RANK), ps["kr"].reshape(1, DB, 1, M_ROPE))
```

```python
import functools
import math

import jax
import jax.numpy as jnp
import numpy as np
from jax import lax
from jax.experimental import pallas as pl
from jax.experimental.pallas import tpu as pltpu

F32 = jnp.float32
BF16 = jnp.bfloat16
I32 = jnp.int32

LANES = 128
EPS = 1e-6
A_HEADS = 8
A_KV_HEADS = 4
A_HEAD_DIM = 64
IDX_HEADS = 8
IDX_DIM = 64
TOPK_MAX = 256
INDEX_SCALE = (IDX_HEADS ** -0.5) * (IDX_DIM ** -0.5)
M_HEADS = 8
M_Q_RANK = 384
M_KV_RANK = 256
M_NOPE = 64
M_ROPE = 32
M_V = 64
MLA_SCALE = (M_NOPE + M_ROPE) ** -0.5
ROPE_BASE = 10000.0
REL_BUCKETS = 32
REL_MAX_DIST = 128
N_GROUPS = 4
EXPERTS_PER_GROUP = 8
N_EXPERTS = 32
D_EXPERT = 256
MOE_BLOCK = 128
Q_BLOCK = 128
PAGE_SIZE = 128
IN_SPLITS = (512, 256, 256, 512, 8, 64, 384, 256, 32, 1024, 1024)
NEG_INF = float("-inf")
VMEM_LIMIT = 56 * 1024 * 1024


def _dot(a, b):
    return jnp.dot(a, b, preferred_element_type=F32)


def _dot_nt(a, b):
    return lax.dot_general(a, b, (((1,), (1,)), ((), ())), preferred_element_type=F32)


def _cparams(sem):
    return pltpu.CompilerParams(dimension_semantics=sem, vmem_limit_bytes=VMEM_LIMIT)


def _expand_heads(z):
    lane = lax.broadcasted_iota(I32, (z.shape[0], LANES), 1)
    lo = lane < 64
    parts = []
    for j in range(z.shape[1] // LANES):
        c = z[:, j * LANES:(j + 1) * LANES]
        parts.append(jnp.where(lo, c, 0.0))
        parts.append(jnp.where(lo, pltpu.roll(c, 64, 1), 0.0))
    return jnp.concatenate(parts, axis=1)


def _rope_slot(r, c_tab, s1_tab, s2_tab):
    return r * c_tab + pltpu.roll(r, LANES - 16, 1) * s1_tab + pltpu.roll(r, 16, 1) * s2_tab


def _proj_kernel(x_ref, ctab_ref, s1tab_ref, s2tab_ref, gattn_ref, w_ref, bgate_ref, e64_ref,
                 gaq_ref, gak_ref, gqa_ref, wuq_ref, gqn_ref, gqr_ref, gkva_ref, gkr_ref,
                 wuk_ref, gkn_ref, wuv_ref,
                 ak_o, av_o, ik_o, ckv_o, kr_o,
                 aqp_o, akp_o, avt_o, iqp_o, iwt_o, ikb_o, qn_o, qr_o, kp_o, vmt_o, ga_o, gb_o):
    x = x_ref[0]
    ms = jnp.mean(x * x, axis=-1, keepdims=True)
    xn = x * lax.rsqrt(ms + EPS) * gattn_ref[...]
    xb = xn.astype(BF16)
    ctab, s1tab, s2tab = ctab_ref[...], s1tab_ref[...], s2tab_ref[...]

    def head_norm(z, width):
        ss = _dot((z * z).astype(BF16), e64_ref[:width, :width])
        return z * lax.rsqrt(ss * (1.0 / 64.0) + EPS)

    z = _dot(xb, w_ref[:, 0:512])
    zn = head_norm(z, 512) * gaq_ref[...] * (A_HEAD_DIM ** -0.5)
    aqp_o[0] = _expand_heads(zn).astype(BF16)
    z = _dot(xb, w_ref[:, 512:768])
    zn = head_norm(z, 256) * gak_ref[...]
    ak_o[0] = zn
    akp_o[0] = _expand_heads(zn).astype(BF16)
    z = _dot(xb, w_ref[:, 768:1024])
    av_o[0] = z
    avt_o[0, 0] = z.T.astype(BF16)
    z = _dot(xb, w_ref[:, 1024:1536])
    iqp_o[0] = _expand_heads(z).astype(BF16)
    z = _dot(xb, w_ref[:, 1536:1664]) * INDEX_SCALE
    iwt_o[0] = z.T[0:IDX_HEADS, :]
    z = _dot(xb, w_ref[:, 1664:1792])
    ik_o[0] = z[:, 0:IDX_DIM]
    ikb_o[0] = z.astype(BF16)
    z = _dot(xb, w_ref[:, 1792:2176])
    cq = z * lax.rsqrt(jnp.mean(z * z, axis=-1, keepdims=True) + EPS) * gqa_ref[...]
    cqb = cq.astype(BF16)
    z = _dot(cqb, wuq_ref[:, 0:512])
    qn_o[0] = (head_norm(z, 512) * gqn_ref[...] * MLA_SCALE).astype(BF16)
    parts = []
    for h in range(M_HEADS):
        r = _dot(cqb, wuq_ref[:, 512 + h * LANES:512 + (h + 1) * LANES])
        ss = jnp.sum(r * r, axis=-1, keepdims=True) * (1.0 / M_ROPE)
        rn = r * lax.rsqrt(ss + EPS) * gqr_ref[...]
        parts.append((_rope_slot(rn, ctab, s1tab, s2tab) * MLA_SCALE).astype(BF16))
    qr_o[0] = jnp.concatenate(parts, axis=1)
    z = _dot(xb, w_ref[:, 2176:2432])
    ckv = z * lax.rsqrt(jnp.mean(z * z, axis=-1, keepdims=True) + EPS) * gkva_ref[...]
    ckv_o[0] = ckv
    cb = ckv.astype(BF16)
    kn = _dot(cb, wuk_ref[...])
    knb = (head_norm(kn, 512) * gkn_ref[...]).astype(BF16)
    vmt_o[0, 0] = _dot(cb, wuv_ref[...]).T.astype(BF16)
    z = _dot(xb, w_ref[:, 2432:2560])
    ss = jnp.sum(z * z, axis=-1, keepdims=True) * (1.0 / M_ROPE)
    kr = _rope_slot(z * lax.rsqrt(ss + EPS) * gkr_ref[...], ctab, s1tab, s2tab)
    kr_o[0] = kr[:, 0:M_ROPE]
    krb = kr.astype(BF16)
    kp_o[0] = jnp.concatenate([p for j in range(4) for p in (knb[:, j * LANES:(j + 1) * LANES], krb)], axis=1)
    z = _dot(xb, w_ref[:, 2560:3584]) + bgate_ref[0:1, :]
    ga_o[0] = jax.nn.sigmoid(z).astype(BF16)
    z = _dot(xb, w_ref[:, 3584:4608]) + bgate_ref[1:2, :]
    gb_o[0] = jax.nn.sigmoid(z).astype(BF16)


def _proj_weights(g_attn, w_in, b_gate, g_a_q, g_a_k, g_m_qa, w_uq, g_m_qn, g_m_qr, g_m_kva, g_m_kr,
                  w_uk, g_m_kn, w_uv):
    D = w_in.shape[0]
    o = [int(v) for v in np.cumsum(IN_SPLITS)]

    def padc(a, width=LANES):
        return jnp.pad(a, ((0, 0), (0, width - a.shape[1])))

    w1 = jnp.concatenate([
        w_in[:, 0:o[3]], padc(w_in[:, o[3]:o[4]]), padc(w_in[:, o[4]:o[5]]), w_in[:, o[5]:o[7]],
        padc(w_in[:, o[7]:o[8]]), w_in[:, o[8]:o[10]]], axis=1).astype(BF16)
    wq = w_uq.reshape(M_Q_RANK, M_HEADS, M_NOPE + M_ROPE)
    wuq = jnp.concatenate(
        [wq[:, :, :M_NOPE].reshape(M_Q_RANK, M_HEADS * M_NOPE)]
        + [padc(wq[:, h, M_NOPE:]) for h in range(M_HEADS)], axis=1).astype(BF16)
    e64 = jnp.asarray(np.kron(np.eye(8, dtype=np.float32), np.ones((64, 64), np.float32)), BF16)
    row = lambda v: v.reshape(1, -1).astype(F32)
    return dict(
        gattn=row(g_attn), w1=w1, bgate=b_gate.astype(F32), e64=e64,
        gaq=row(jnp.tile(g_a_q, A_HEADS)), gak=row(jnp.tile(g_a_k, A_KV_HEADS)), gqa=row(g_m_qa), wuq=wuq,
        gqn=row(jnp.tile(g_m_qn, M_HEADS)), gqr=padc(row(g_m_qr)), gkva=row(g_m_kva), gkr=padc(row(g_m_kr)),
        wuk=w_uk.reshape(M_KV_RANK, M_HEADS * M_NOPE).astype(BF16), gkn=row(jnp.tile(g_m_kn, M_HEADS)),
        wuv=w_uv.reshape(M_KV_RANK, M_HEADS * M_V).astype(BF16))


def _rope_tables(pos):
    half = M_ROPE // 2
    inv = ROPE_BASE ** (-jnp.arange(half, dtype=F32) / half)
    ang = pos.astype(F32)[:, None] * inv
    cos, sin = jnp.cos(ang), jnp.sin(ang)
    n = pos.shape[0]
    ctab = jnp.concatenate([cos, cos, jnp.ones((n, LANES - 2 * half), F32)], axis=1)
    s1tab = jnp.concatenate([-sin, jnp.zeros((n, LANES - half), F32)], axis=1)
    s2tab = jnp.concatenate([jnp.zeros((n, half), F32), sin, jnp.zeros((n, LANES - 2 * half), F32)], axis=1)
    return ctab, s1tab, s2tab


def _proj(x, pos, pw, tm):
    B, S, D = x.shape
    ctab, s1tab, s2tab = _rope_tables(pos)
    tok = lambda w: pl.BlockSpec((1, tm, w), lambda s, b: (b, s, 0))
    tokT = lambda w: pl.BlockSpec((1, w, tm), lambda s, b: (b, 0, s))
    tokC = lambda w: pl.BlockSpec((1, 1, w, tm), lambda s, b: (b, s, 0, 0))
    tab = pl.BlockSpec((tm, LANES), lambda s, b: (s, 0))
    full = lambda a: pl.BlockSpec(a.shape, lambda s, b: (0,) * a.ndim)
    wnames = ["gattn", "w1", "bgate", "e64", "gaq", "gak", "gqa", "wuq", "gqn", "gqr", "gkva", "gkr", "wuk", "gkn",
              "wuv"]
    ws = [pw[n] for n in wnames]
    sds = lambda shape, dt: jax.ShapeDtypeStruct(shape, dt)
    out_shape = [
        sds((B, S, 256), F32), sds((B, S, 256), F32), sds((B, S, IDX_DIM), F32), sds((B, S, M_KV_RANK), F32),
        sds((B, S, M_ROPE), F32),
        sds((B, S, 1024), BF16), sds((B, S, 512), BF16), sds((B, S // tm, 256, tm), BF16), sds((B, S, 1024), BF16),
        sds((B, IDX_HEADS, S), F32), sds((B, S, LANES), BF16), sds((B, S, 512), BF16), sds((B, S, 1024), BF16),
        sds((B, S, 1024), BF16), sds((B, S // tm, 512, tm), BF16), sds((B, S, D), BF16), sds((B, S, D), BF16)]
    out_specs = [tok(256), tok(256), tok(IDX_DIM), tok(M_KV_RANK), tok(M_ROPE),
                 tok(1024), tok(512), tokC(256), tok(1024), tokT(IDX_HEADS), tok(LANES), tok(512), tok(1024),
                 tok(1024), tokC(512), tok(D), tok(D)]
    outs = pl.pallas_call(
        _proj_kernel, out_shape=out_shape, grid=(S // tm, B),
        in_specs=[tok(D), tab, tab, tab] + [full(a) for a in ws], out_specs=out_specs,
        compiler_params=_cparams(("arbitrary", "arbitrary")), name="proj")(x, ctab, s1tab, s2tab, *ws)
    names = ["ak", "av", "ik", "ckv", "kr", "aqp", "akp", "avT", "iqp", "iwT", "ikb", "qn", "qr", "kp", "vmT",
             "ga", "gb"]
    return dict(zip(names, outs))


INT_MIN = -2 ** 31


def _rel_bucket(dist):
    n = jnp.maximum(dist, 0)
    max_exact = REL_BUCKETS // 2
    nf = jnp.maximum(n, 1).astype(F32)
    large = max_exact + (jnp.log(nf / max_exact) / math.log(REL_MAX_DIST / max_exact)
                         * (REL_BUCKETS - max_exact)).astype(I32)
    large = jnp.minimum(large, REL_BUCKETS - 1)
    return jnp.where(n < max_exact, n, large)


def _bias_tiles(rel_bias):
    qq = jnp.arange(Q_BLOCK, dtype=I32)[None, :]
    kk = jnp.arange(Q_BLOCK, dtype=I32)[:, None]
    tiles = [jnp.moveaxis(rel_bias[_rel_bucket(o * Q_BLOCK + qq - kk)], -1, 0) for o in range(3)]
    return jnp.stack(tiles).astype(F32)


def _sortable_key(sc):
    bits = pltpu.bitcast(sc, I32)
    key = jnp.where(bits < 0, bits ^ jnp.int32(0x7FFFFFFF), bits)
    return jnp.where(bits == jnp.int32(INT_MIN), 0, key)


def _colsum8(m):
    return m.reshape(m.shape[0] // 8, 8, LANES).sum(axis=0)


def _select_topk_columns(key_ref, n_chunks, kb, topk, active, idx_bits):
    def count(pred):
        def body(c, acc):
            r0 = pl.multiple_of(c * kb, kb)
            k = key_ref[pl.ds(r0, kb), :]
            return acc + _colsum8(pred(k, r0).astype(I32))
        acc = lax.fori_loop(0, n_chunks, body, jnp.zeros((8, LANES), I32))
        return jnp.sum(acc, axis=0, keepdims=True)

    zero = jnp.zeros((1, LANES), I32)
    cnt0 = count(lambda k, r0: k >= zero)
    t0 = jnp.where(jnp.logical_and(cnt0 >= topk, active), 0, INT_MIN).astype(I32)

    def bit_body(j, t):
        cand = t | jnp.left_shift(jnp.int32(1), 30 - j)
        cnt = count(lambda k, r0: k >= cand)
        return jnp.where(cnt >= topk, cand, t)

    t = lax.fori_loop(0, jnp.where(active, 31, 0), bit_body, t0)
    need = topk - count(lambda k, r0: k > t)
    excess = jnp.max(count(lambda k, r0: k == t) - need) > 0
    tie_search = jnp.logical_and(active, excess)

    def tie_body(j, cut):
        cand = cut | jnp.left_shift(jnp.int32(1), idx_bits - 1 - j)

        def pred(k, r0):
            row = r0 + lax.broadcasted_iota(I32, (kb, LANES), 0)
            return jnp.logical_and(k == t, row < cand)
        return jnp.where(count(pred) <= need, cand, cut)

    cut0 = jnp.where(tie_search, zero, jnp.int32(2 ** idx_bits - 1))
    cut = lax.fori_loop(0, jnp.where(tie_search, idx_bits, 0), tie_body, cut0)
    return t, cut


def _softmax_step(s, m_old, l_old):
    m_new = jnp.maximum(m_old, jnp.max(s, axis=0, keepdims=True))
    m_safe = jnp.where(m_new == NEG_INF, 0.0, m_new)
    p = jnp.exp(s - m_safe)
    alpha = jnp.exp(m_old - m_safe)
    l_new = alpha * l_old + jnp.sum(p, axis=0, keepdims=True)
    return p, alpha, m_new, l_new


def _dsa_prompt_kernel(aq_ref, iq_ref, iw_ref, ak_ref, ik_ref, avt_ref, bias_ref, y_ref,
                       sc_ref, key_ref, acc_ref, m_ref, l_ref, *, topk, kb, idx_bits):
    i = pl.program_id(1)
    n_keys = (i + 1) * Q_BLOCK
    n_chunks = (n_keys + kb - 1) // kb
    q_pos = i * Q_BLOCK + lax.broadcasted_iota(I32, (kb, LANES), 1)
    row_iota = lax.broadcasted_iota(I32, (kb, LANES), 0)

    iq = iq_ref[0]
    iq_stack = jnp.concatenate([iq[:, h * LANES:(h + 1) * LANES] for h in range(IDX_HEADS)], axis=0)
    iw = iw_ref[0]

    def score_body(c, carry):
        r0 = pl.multiple_of(c * kb, kb)
        d = _dot_nt(ik_ref[0, pl.ds(r0, kb), :], iq_stack)
        sc = iw[0:1, :] * jnp.maximum(d[:, 0:LANES], 0.0)
        for h in range(1, IDX_HEADS):
            sc = sc + iw[h:h + 1, :] * jnp.maximum(d[:, h * LANES:(h + 1) * LANES], 0.0)
        sc = jnp.where(r0 + row_iota <= q_pos, sc, NEG_INF)
        key_ref[pl.ds(r0, kb), :] = _sortable_key(sc)
        return carry

    lax.fori_loop(0, n_chunks, score_body, 0)
    t, cut = _select_topk_columns(key_ref, n_chunks, kb, topk, n_keys > topk, idx_bits)

    def mask_body(c, carry):
        r0 = pl.multiple_of(c * kb, kb)
        k = key_ref[pl.ds(r0, kb), :]
        row = r0 + row_iota
        sel = jnp.logical_or(k > t, jnp.logical_and(k == t, row < cut))
        sc_ref[pl.ds(r0, kb), :] = jnp.where(jnp.logical_and(sel, row <= q_pos), 0.0, NEG_INF)
        return carry

    lax.fori_loop(0, n_chunks, mask_body, 0)

    aq = aq_ref[0]
    aq_stack = [jnp.concatenate([aq[:, (2 * g) * LANES:(2 * g + 1) * LANES],
                                 aq[:, (2 * g + 1) * LANES:(2 * g + 2) * LANES]], axis=0)
                for g in range(A_KV_HEADS)]
    m_ref[...] = jnp.full(m_ref.shape, NEG_INF, F32)
    l_ref[...] = jnp.zeros(l_ref.shape, F32)
    acc_ref[...] = jnp.zeros(acc_ref.shape, F32)
    n_sub = kb // Q_BLOCK

    def scores(c):
        r0 = pl.multiple_of(c * kb, kb)
        return [_dot_nt(ak_ref[0, pl.ds(r0, kb), g * LANES:(g + 1) * LANES], aq_stack[g])
                for g in range(A_KV_HEADS)]

    def attn_body(c, s_cur):
        s_next = scores(jnp.minimum(c + 1, n_chunks - 1))
        r0 = pl.multiple_of(c * kb, kb)
        addm = sc_ref[pl.ds(r0, kb), :]
        tile_idx = [jnp.clip(i - (c * n_sub + u), 0, 2) for u in range(n_sub)]

        def softmax_and_pv(g, s_pair):
            ps, alphas = [], []
            for hh in range(2):
                h = 2 * g + hh
                bias = jnp.concatenate([bias_ref[tile_idx[u], h] for u in range(n_sub)], axis=0)
                s = s_pair[:, hh * LANES:(hh + 1) * LANES] + bias + addm
                p, alpha, m_new, l_new = _softmax_step(s, m_ref[h:h + 1, :], l_ref[h:h + 1, :])
                m_ref[h:h + 1, :] = m_new
                l_ref[h:h + 1, :] = l_new
                ps.append(p.astype(BF16))
                alphas.append(alpha)
            o_pair = _dot(avt_ref[0, c, g * A_HEAD_DIM:(g + 1) * A_HEAD_DIM, :], jnp.concatenate(ps, axis=1))
            for hh in range(2):
                h = 2 * g + hh
                rows = slice(h * A_HEAD_DIM, (h + 1) * A_HEAD_DIM)
                acc_ref[rows, :] = alphas[hh] * acc_ref[rows, :] + o_pair[:, hh * LANES:(hh + 1) * LANES]

        for g in range(A_KV_HEADS):
            softmax_and_pv(g, s_cur[g])
        return s_next

    lax.fori_loop(0, n_chunks, attn_body, scores(0))
    outs = [acc_ref[h * A_HEAD_DIM:(h + 1) * A_HEAD_DIM, :] / l_ref[h:h + 1, :] for h in range(A_HEADS)]
    y_ref[0] = jnp.concatenate(outs, axis=0).T.astype(BF16)


def _dsa_prompt(p, bias_tiles, kb):
    B, S, _ = p["aqp"].shape
    topk = min(TOPK_MAX, S // 4)
    nqb = S // Q_BLOCK
    idx_bits = int(S).bit_length()
    qblk = lambda w: pl.BlockSpec((1, Q_BLOCK, w), lambda b, i: (b, i, 0))
    kern = functools.partial(_dsa_prompt_kernel, topk=topk, kb=kb, idx_bits=idx_bits)
    return pl.pallas_call(
        kern, out_shape=jax.ShapeDtypeStruct((B, S, A_HEADS * A_HEAD_DIM), BF16), grid=(B, nqb),
        in_specs=[qblk(1024), qblk(1024), pl.BlockSpec((1, IDX_HEADS, Q_BLOCK), lambda b, i: (b, 0, i)),
                  pl.BlockSpec((1, S, 512), lambda b, i: (b, 0, 0)),
                  pl.BlockSpec((1, S, LANES), lambda b, i: (b, 0, 0)),
                  pl.BlockSpec((1, S // kb, 256, kb), lambda b, i: (b, 0, 0, 0)),
                  pl.BlockSpec(bias_tiles.shape, lambda b, i: (0, 0, 0, 0))],
        out_specs=qblk(A_HEADS * A_HEAD_DIM),
        scratch_shapes=[pltpu.VMEM((S, LANES), F32), pltpu.VMEM((S, LANES), I32),
                        pltpu.VMEM((A_HEADS * A_HEAD_DIM, LANES), F32), pltpu.VMEM((A_HEADS, LANES), F32),
                        pltpu.VMEM((A_HEADS, LANES), F32)],
        compiler_params=_cparams(("arbitrary", "arbitrary")), name="dsa_prompt",
    )(p["aqp"], p["iqp"], p["iwT"], p["akp"], p["ikb"], p["avT"], bias_tiles)


def _causal_tiles():
    qq = np.arange(Q_BLOCK)[None, :]
    kk = np.arange(Q_BLOCK)[:, None]
    tri = np.where(kk <= qq, 0.0, -np.inf).astype(np.float32)
    return jnp.asarray(np.stack([np.zeros_like(tri), tri, np.full_like(tri, -np.inf)]))


def _mla_prompt_kernel(qn_ref, qr_ref, kp_ref, vmt_ref, mask_ref, y_ref, acc_ref, m_ref, l_ref, *, kb):
    i = pl.program_id(1)
    n_chunks = ((i + 1) * Q_BLOCK + kb - 1) // kb
    n_sub = kb // Q_BLOCK
    lo = lax.broadcasted_iota(I32, (Q_BLOCK, LANES), 1) < M_NOPE
    qn, qr = qn_ref[0], qr_ref[0]
    zero = jnp.zeros((Q_BLOCK, LANES), BF16)
    q_stack = []
    for j in range(M_HEADS // 2):
        c = qn[:, j * LANES:(j + 1) * LANES]
        top = jnp.concatenate([jnp.where(lo, c, zero), qr[:, (2 * j) * LANES:(2 * j + 1) * LANES]], axis=1)
        bot = jnp.concatenate([jnp.where(lo, zero, c), qr[:, (2 * j + 1) * LANES:(2 * j + 2) * LANES]], axis=1)
        q_stack.append(jnp.concatenate([top, bot], axis=0))
    m_ref[...] = jnp.full(m_ref.shape, NEG_INF, F32)
    l_ref[...] = jnp.zeros(l_ref.shape, F32)
    acc_ref[...] = jnp.zeros(acc_ref.shape, F32)

    def scores(c):
        r0 = pl.multiple_of(c * kb, kb)
        return [_dot_nt(kp_ref[0, pl.ds(r0, kb), j * 2 * LANES:(j + 1) * 2 * LANES], q_stack[j])
                for j in range(M_HEADS // 2)]

    def attn_body(c, s_cur):
        s_next = scores(jnp.minimum(c + 1, n_chunks - 1))
        addm = jnp.concatenate([mask_ref[jnp.clip(c * n_sub + u - i + 1, 0, 2)] for u in range(n_sub)], axis=0)
        for h in range(M_HEADS):
            s = s_cur[h // 2][:, (h % 2) * LANES:(h % 2 + 1) * LANES] + addm
            p, alpha, m_new, l_new = _softmax_step(s, m_ref[h:h + 1, :], l_ref[h:h + 1, :])
            m_ref[h:h + 1, :] = m_new
            l_ref[h:h + 1, :] = l_new
            rows = slice(h * M_V, (h + 1) * M_V)
            o = _dot(vmt_ref[0, c, rows, :], p.astype(BF16))
            acc_ref[rows, :] = alpha * acc_ref[rows, :] + o
        return s_next

    lax.fori_loop(0, n_chunks, attn_body, scores(0))
    outs = [acc_ref[h * M_V:(h + 1) * M_V, :] / l_ref[h:h + 1, :] for h in range(M_HEADS)]
    y_ref[0] = jnp.concatenate(outs, axis=0).T.astype(BF16)


def _mla_prompt(p, kb):
    B, S, _ = p["qn"].shape
    qblk = lambda w: pl.BlockSpec((1, Q_BLOCK, w), lambda b, i: (b, i, 0))
    masks = _causal_tiles()
    return pl.pallas_call(
        functools.partial(_mla_prompt_kernel, kb=kb),
        out_shape=jax.ShapeDtypeStruct((B, S, M_HEADS * M_V), BF16), grid=(B, S // Q_BLOCK),
        in_specs=[qblk(512), qblk(1024), pl.BlockSpec((1, S, 1024), lambda b, i: (b, 0, 0)),
                  pl.BlockSpec((1, S // kb, 512, kb), lambda b, i: (b, 0, 0, 0)),
                  pl.BlockSpec(masks.shape, lambda b, i: (0, 0, 0))],
        out_specs=qblk(M_HEADS * M_V),
        scratch_shapes=[pltpu.VMEM((M_HEADS * M_V, LANES), F32), pltpu.VMEM((M_HEADS, LANES), F32),
                        pltpu.VMEM((M_HEADS, LANES), F32)],
        compiler_params=_cparams(("arbitrary", "arbitrary")), name="mla_prompt",
    )(p["qn"], p["qr"], p["kp"], p["vmT"], masks)


PAGES_PER_STEP = 8


def _page_specs(shape_tail, n_pages, per_step=PAGES_PER_STEP):
    def spec(u):
        def index_map(b, j, pt):
            return (0, pt[b * n_pages + j * per_step + u]) + (0,) * len(shape_tail)
        return pl.BlockSpec((1, 1) + tuple(shape_tail), index_map)
    return [spec(u) for u in range(per_step)]


def _lane_column(mat, b):
    lane = lax.broadcasted_iota(I32, mat.shape, 1)
    return jnp.sum(jnp.where(lane == b, mat, 0.0), axis=1, keepdims=True)


IDX_PAGES_PER_STEP = 16


def _idx_sample_kernel(pt_ref, iq_ref, iwt_ref, iknew_ref, *rest):
    del pt_ref
    pages = rest[:IDX_PAGES_PER_STEP]
    sc_o, scnew_o = rest[IDX_PAGES_PER_STEP:]
    b, j = pl.program_id(0), pl.program_id(1)
    iq = iq_ref[0][:, 0:IDX_DIM]
    wcol = _lane_column(iwt_ref[...], b)

    def scores(keys_t):
        d = _dot(iq, keys_t.astype(BF16))
        return jnp.sum(wcol * jnp.maximum(d, 0.0), axis=0, keepdims=True)

    sc_o[0] = scores(jnp.concatenate([pages[u][0, 0] for u in range(IDX_PAGES_PER_STEP)], axis=1))

    @pl.when(j == 0)
    def _():
        lane = lax.broadcasted_iota(I32, (1, LANES), 1)
        scnew_o[0] = jnp.where(lane == 0, scores(iknew_ref[0]), NEG_INF)


def _idx_sample(pt_flat, iq3, iwT, iknew_page_t, cache_idx_t, n_pages):
    DB = iq3.shape[0]
    per_step = IDX_PAGES_PER_STEP
    grid_spec = pltpu.PrefetchScalarGridSpec(
        num_scalar_prefetch=1, grid=(DB, n_pages // per_step),
        in_specs=[pl.BlockSpec((1, IDX_HEADS, LANES), lambda b, j, pt: (b, 0, 0)),
                  pl.BlockSpec(iwT.shape, lambda b, j, pt: (0, 0)),
                  pl.BlockSpec((1, IDX_DIM, PAGE_SIZE), lambda b, j, pt: (b, 0, 0))]
        + _page_specs((IDX_DIM, PAGE_SIZE), n_pages, per_step),
        out_specs=[pl.BlockSpec((1, 1, per_step * PAGE_SIZE), lambda b, j, pt: (b, 0, j)),
                   pl.BlockSpec((1, 1, LANES), lambda b, j, pt: (b, 0, 0))])
    return pl.pallas_call(
        _idx_sample_kernel,
        out_shape=[jax.ShapeDtypeStruct((DB, 1, n_pages * PAGE_SIZE), F32), jax.ShapeDtypeStruct((DB, 1, LANES), F32)],
        grid_spec=grid_spec, compiler_params=_cparams(("arbitrary", "arbitrary")), name="idx_sample",
    )(pt_flat, iq3, iwT, iknew_page_t, *([cache_idx_t] * per_step))


def _sel_sample_kernel(sc_ref, sel_ref, key_ref, *, topk, kb, idx_bits):
    n_rows = sc_ref.shape[0]
    n_chunks = n_rows // kb
    row_iota = lax.broadcasted_iota(I32, (kb, LANES), 0)

    def key_body(c, carry):
        r0 = pl.multiple_of(c * kb, kb)
        key_ref[pl.ds(r0, kb), :] = _sortable_key(sc_ref[pl.ds(r0, kb), :])
        return carry

    lax.fori_loop(0, n_chunks, key_body, 0)
    t, cut = _select_topk_columns(key_ref, n_chunks, kb, topk, jnp.bool_(True), idx_bits)

    def mask_body(c, carry):
        r0 = pl.multiple_of(c * kb, kb)
        k = key_ref[pl.ds(r0, kb), :]
        sel = jnp.logical_or(k > t, jnp.logical_and(k == t, r0 + row_iota < cut))
        sel_ref[pl.ds(r0, kb), :] = sel.astype(F32)
        return carry

    lax.fori_loop(0, n_chunks, mask_body, 0)


def _sel_sample(scores_t, topk):
    L, DB = scores_t.shape
    kern = functools.partial(_sel_sample_kernel, topk=topk, kb=PAGE_SIZE, idx_bits=int(L).bit_length())
    return pl.pallas_call(
        kern, out_shape=jax.ShapeDtypeStruct((L, DB), F32), grid=(1,),
        in_specs=[pl.BlockSpec((L, DB), lambda i: (0, 0))], out_specs=pl.BlockSpec((L, DB), lambda i: (0, 0)),
        scratch_shapes=[pltpu.VMEM((L, DB), I32)],
        compiler_params=_cparams(("arbitrary",)), name="sel_sample")(scores_t)


def _dsa_sample_kernel(pt_ref, aq_ref, sel_ref, selnew_ref, knew_ref, vnew_ref, bias_ref, bnew_ref, *rest, n_steps):
    del pt_ref
    pps = PAGES_PER_STEP
    k_pages, v_pages = rest[:pps], rest[pps:2 * pps]
    y_o, acc_ref, m_ref, l_ref = rest[2 * pps:]
    j = pl.program_id(1)
    kv_of_row = lax.broadcasted_iota(I32, (A_HEADS, LANES), 0) // (A_HEADS // A_KV_HEADS)
    aq = aq_ref[0]
    q64 = aq[:, 0:A_HEAD_DIM]

    def per_kv_head(pieces):
        out = pieces[A_KV_HEADS - 1]
        for g in range(A_KV_HEADS - 2, -1, -1):
            out = jnp.where(kv_of_row[:, 0:1] == g, pieces[g], out)
        return out

    @pl.when(j == 0)
    def _():
        kn = knew_ref[0].astype(BF16).astype(F32)
        vn = vnew_ref[0].astype(BF16).astype(F32)
        lane = lax.broadcasted_iota(I32, (1, LANES), 1)
        picked = jnp.sum(jnp.where(lane == 0, selnew_ref[0], 0.0), axis=1, keepdims=True) > 0.0
        qf = aq.astype(F32)
        halves = [kn[:, 0:LANES], pltpu.roll(kn[:, 0:LANES], 64, 1), kn[:, LANES:], pltpu.roll(kn[:, LANES:], 64, 1)]
        lo = lax.broadcasted_iota(I32, (A_HEADS, LANES), 1) < A_HEAD_DIM
        s_new = jnp.sum(jnp.where(lo, qf * per_kv_head(halves), 0.0), axis=1, keepdims=True) + bnew_ref[...]
        vh = [vn[:, 0:LANES], pltpu.roll(vn[:, 0:LANES], 64, 1), vn[:, LANES:], pltpu.roll(vn[:, LANES:], 64, 1)]
        m_ref[...] = jnp.where(picked, s_new, NEG_INF)
        l_ref[...] = jnp.where(picked, 1.0, 0.0) + jnp.zeros(l_ref.shape, F32)
        acc_ref[...] = jnp.where(picked, per_kv_head(vh)[:, 0:A_HEAD_DIM] + jnp.zeros(acc_ref.shape, F32), 0.0)

    kcat = [jnp.concatenate([k_pages[u][0, 0, g] for u in range(pps)], axis=1).astype(BF16) for g in range(A_KV_HEADS)]
    s = per_kv_head([_dot(q64, kcat[g]) for g in range(A_KV_HEADS)])
    last = j == n_steps - 1
    bias = jnp.concatenate([bias_ref[1]] * (pps - 1) + [bias_ref[jnp.where(last, 0, 1)]], axis=1)
    keep = jnp.concatenate([sel_ref[0, u:u + 1, :] for u in range(pps)], axis=1) > 0.0
    s = jnp.where(keep, s + bias, NEG_INF)
    m_old = m_ref[...]
    m_new = jnp.maximum(m_old, jnp.max(s, axis=1, keepdims=True))
    m_safe = jnp.where(m_new == NEG_INF, 0.0, m_new)
    p = jnp.exp(s - m_safe)
    alpha = jnp.exp(m_old - m_safe)
    l_ref[...] = alpha * l_ref[...] + jnp.sum(p, axis=1, keepdims=True)
    m_ref[...] = m_new
    pb = p.astype(BF16)
    vcat = [jnp.concatenate([v_pages[u][0, 0, g] for u in range(pps)], axis=1).astype(BF16) for g in range(A_KV_HEADS)]
    acc_ref[...] = alpha * acc_ref[...] + per_kv_head([_dot_nt(pb, vcat[g]) for g in range(A_KV_HEADS)])

    @pl.when(last)
    def _():
        y_o[0] = acc_ref[...] / l_ref[...]


def _dsa_sample(pt_flat, aq3, sel_past, sel_new, knew, vnew, bias2, bias_new, ck_t, cv_t, n_pages):
    DB = aq3.shape[0]
    n_steps = n_pages // PAGES_PER_STEP
    page_tail = (A_KV_HEADS, A_HEAD_DIM, PAGE_SIZE)
    grid_spec = pltpu.PrefetchScalarGridSpec(
        num_scalar_prefetch=1, grid=(DB, n_steps),
        in_specs=[pl.BlockSpec((1, A_HEADS, LANES), lambda b, j, pt: (b, 0, 0)),
                  pl.BlockSpec((1, PAGES_PER_STEP, LANES), lambda b, j, pt: (b, j, 0)),
                  pl.BlockSpec((1, 1, LANES), lambda b, j, pt: (b, 0, 0)),
                  pl.BlockSpec((1, 1, 256), lambda b, j, pt: (b, 0, 0)),
                  pl.BlockSpec((1, 1, 256), lambda b, j, pt: (b, 0, 0)),
                  pl.BlockSpec(bias2.shape, lambda b, j, pt: (0, 0, 0)),
                  pl.BlockSpec(bias_new.shape, lambda b, j, pt: (0, 0))]
        + _page_specs(page_tail, n_pages) + _page_specs(page_tail, n_pages),
        out_specs=pl.BlockSpec((1, A_HEADS, A_HEAD_DIM), lambda b, j, pt: (b, 0, 0)),
        scratch_shapes=[pltpu.VMEM((A_HEADS, A_HEAD_DIM), F32), pltpu.VMEM((A_HEADS, 1), F32),
                        pltpu.VMEM((A_HEADS, 1), F32)])
    return pl.pallas_call(
        functools.partial(_dsa_sample_kernel, n_steps=n_steps),
        out_shape=jax.ShapeDtypeStruct((DB, A_HEADS, A_HEAD_DIM), F32), grid_spec=grid_spec,
        compiler_params=_cparams(("arbitrary", "arbitrary")), name="dsa_sample",
    )(pt_flat, aq3, sel_past, sel_new, knew, vnew, bias2, bias_new,
      *([ck_t] * PAGES_PER_STEP), *([cv_t] * PAGES_PER_STEP))


def _mla_sample_kernel(pt_ref, qgt_ref, qr_ref, latnew_ref, krnew_ref, wukt_ref, wuv_ref, *rest,
                       n_steps):
    del pt_ref
    lat_pages = rest[:PAGES_PER_STEP]
    kr_pages = rest[PAGES_PER_STEP:2 * PAGES_PER_STEP]
    y_o, qcol_ref, acc_ref, m_ref, l_ref = rest[2 * PAGES_PER_STEP:]
    b, j = pl.program_id(0), pl.program_id(1)
    qr = qr_ref[0][:, 0:M_ROPE]

    def attend(lat, kr_t, valid):
        nk = lat.shape[0]
        latb = lat.astype(BF16)
        knt = _dot_nt(wukt_ref[...], latb)
        qcol = qcol_ref[...]
        rows = []
        for h in range(M_HEADS):
            kh = knt[h * M_NOPE:(h + 1) * M_NOPE, :]
            qh = qcol[h * M_NOPE:(h + 1) * M_NOPE, :]
            if nk > LANES:
                qh = jnp.concatenate([qh] * (nk // LANES), axis=1)
            ss = jnp.sum(kh * kh, axis=0, keepdims=True) * (1.0 / M_NOPE)
            raw = jnp.sum(kh * qh, axis=0, keepdims=True)
            rows.append(raw * lax.rsqrt(ss + EPS))
        s = jnp.concatenate(rows, axis=0) + _dot(qr, kr_t.astype(BF16))
        if valid is not None:
            s = jnp.where(valid, s, NEG_INF)
        m_old = m_ref[...]
        m_new = jnp.maximum(m_old, jnp.max(s, axis=1, keepdims=True))
        m_safe = jnp.where(m_new == NEG_INF, 0.0, m_new)
        p = jnp.exp(s - m_safe)
        alpha = jnp.exp(m_old - m_safe)
        l_ref[...] = alpha * l_ref[...] + jnp.sum(p, axis=1, keepdims=True)
        m_ref[...] = m_new
        acc_ref[...] = alpha * acc_ref[...] + _dot(p.astype(BF16), latb)

    @pl.when(j == 0)
    def _():
        qcol_ref[...] = jnp.zeros(qcol_ref.shape, F32) + _lane_column(qgt_ref[...], b)
        m_ref[...] = jnp.full(m_ref.shape, NEG_INF, F32)
        l_ref[...] = jnp.zeros(l_ref.shape, F32)
        acc_ref[...] = jnp.zeros(acc_ref.shape, F32)
        lane = lax.broadcasted_iota(I32, (M_HEADS, LANES), 1)
        attend(latnew_ref[0], krnew_ref[0], lane == 0)

    attend(jnp.concatenate([lat_pages[u][0, 0] for u in range(PAGES_PER_STEP)], axis=0),
           jnp.concatenate([kr_pages[u][0, 0] for u in range(PAGES_PER_STEP)], axis=1), None)

    @pl.when(j == n_steps - 1)
    def _():
        lat_out = (acc_ref[...] / l_ref[...]).astype(BF16)
        full = _dot(lat_out, wuv_ref[...])
        head = lax.broadcasted_iota(I32, full.shape, 1) // M_V
        row = lax.broadcasted_iota(I32, full.shape, 0)
        y_o[0] = jnp.sum(jnp.where(head == row, full, 0.0), axis=0, keepdims=True)


def _mla_sample(pt_flat, qgt, qr3, latnew, krnew, wukt, wuv, cache_lat, cache_kr, n_pages):
    DB = qr3.shape[0]
    n_steps = n_pages // PAGES_PER_STEP
    grid_spec = pltpu.PrefetchScalarGridSpec(
        num_scalar_prefetch=1, grid=(DB, n_steps),
        in_specs=[pl.BlockSpec(qgt.shape, lambda b, j, pt: (0, 0)),
                  pl.BlockSpec((1, M_HEADS, LANES), lambda b, j, pt: (b, 0, 0)),
                  pl.BlockSpec((1, PAGE_SIZE, M_KV_RANK), lambda b, j, pt: (b, 0, 0)),
                  pl.BlockSpec((1, M_ROPE, PAGE_SIZE), lambda b, j, pt: (b, 0, 0)),
                  pl.BlockSpec(wukt.shape, lambda b, j, pt: (0, 0)),
                  pl.BlockSpec(wuv.shape, lambda b, j, pt: (0, 0))]
        + _page_specs((PAGE_SIZE, M_KV_RANK), n_pages) + _page_specs((M_ROPE, PAGE_SIZE), n_pages),
        out_specs=pl.BlockSpec((1, 1, M_HEADS * M_V), lambda b, j, pt: (b, 0, 0)),
        scratch_shapes=[pltpu.VMEM((M_HEADS * M_NOPE, LANES), F32), pltpu.VMEM((M_HEADS, M_KV_RANK), F32),
                        pltpu.VMEM((M_HEADS, 1), F32), pltpu.VMEM((M_HEADS, 1), F32)])
    return pl.pallas_call(
        functools.partial(_mla_sample_kernel, n_steps=n_steps),
        out_shape=jax.ShapeDtypeStruct((DB, 1, M_HEADS * M_V), F32), grid_spec=grid_spec,
        compiler_params=_cparams(("arbitrary", "arbitrary")), name="mla_sample",
    )(pt_flat, qgt, qr3, latnew, krnew, wukt, wuv, *([cache_lat] * PAGES_PER_STEP), *([cache_kr] * PAGES_PER_STEP))


ROUTER_ROWS = 40


def _merge_kernel(x_ref, ya_ref, yb_ref, ga_ref, gb_ref, woa_ref, wob_ref, wout_ref, gffn_ref,
                  wrh_ref, wrl_ref, br_ref, tri_ref,
                  h_o, hn_o, eid_o, rank_o, wcol_o, cnt_o, cnt_ref):
    tm = x_ref.shape[0]

    @pl.when(pl.program_id(0) == 0)
    def _():
        cnt_ref[...] = jnp.zeros(cnt_ref.shape, F32)

    merged = (ga_ref[...].astype(F32) * _dot(ya_ref[...], woa_ref[...])
              + gb_ref[...].astype(F32) * _dot(yb_ref[...], wob_ref[...]))
    h = x_ref[...] + _dot(merged.astype(BF16), wout_ref[...])
    h_o[...] = h
    hn = h * lax.rsqrt(jnp.mean(h * h, axis=-1, keepdims=True) + EPS) * gffn_ref[...]
    hn_o[...] = hn
    hh = hn.astype(BF16)
    hl = (hn - hh.astype(F32)).astype(BF16)
    logit = (_dot_nt(wrh_ref[...], hh) + _dot_nt(wrh_ref[...], hl) + _dot_nt(wrl_ref[...], hh)) + br_ref[...]
    row8 = lax.broadcasted_iota(I32, (8, tm), 0)
    gl = logit[0:8, :]
    gmax = jnp.max(gl, axis=0, keepdims=True)
    gtop = jnp.min(jnp.where(gl == gmax, row8, 8), axis=0, keepdims=True)
    pg = 1.0 / jnp.sum(jnp.exp(gl - gmax), axis=0, keepdims=True)
    elg = logit[8:16, :]
    for g in range(1, N_GROUPS):
        elg = jnp.where(gtop == g, logit[8 + 8 * g:16 + 8 * g, :], elg)
    m1 = jnp.max(elg, axis=0, keepdims=True)
    i1 = jnp.min(jnp.where(elg == m1, row8, 8), axis=0, keepdims=True)
    elg2 = jnp.where(row8 == i1, NEG_INF, elg)
    m2 = jnp.max(elg2, axis=0, keepdims=True)
    i2 = jnp.min(jnp.where(elg2 == m2, row8, 8), axis=0, keepdims=True)
    e2 = jnp.exp(m2 - m1)
    den = 1.0 + e2
    w0 = (1.0 / den) * pg
    w1 = (e2 / den) * pg
    eid0 = gtop * EXPERTS_PER_GROUP + i1
    eid1 = gtop * EXPERTS_PER_GROUP + i2
    erow = lax.broadcasted_iota(I32, (N_EXPERTS, tm), 0)
    oh0 = erow == eid0
    oh1 = erow == eid1
    ohs = oh0.astype(F32) + oh1.astype(F32)
    base = cnt_ref[...] + _dot(ohs.astype(BF16), tri_ref[...])
    rank0 = jnp.sum(jnp.where(oh0, base, 0.0), axis=0, keepdims=True)
    rank1 = jnp.sum(jnp.where(oh1, base, 0.0), axis=0, keepdims=True)
    cnt_ref[...] = cnt_ref[...] + jnp.sum(ohs, axis=1, keepdims=True)
    cnt_o[...] = cnt_ref[...]
    eid_o[...] = jnp.concatenate([eid0, eid1], axis=0)
    rank_o[...] = jnp.concatenate([rank0, rank1], axis=0).astype(I32)
    wcol_o[...] = jnp.concatenate([w0, w1, jnp.zeros((LANES - 2, tm), F32)], axis=0).T


def _merge(x, ya, yb, ga, gb, mw, tm):
    N, D = x.shape
    tok = lambda w: pl.BlockSpec((tm, w), lambda i: (i, 0))
    rows = lambda r: pl.BlockSpec((r, tm), lambda i: (0, i))
    full = lambda a: pl.BlockSpec(a.shape, lambda i: (0,) * a.ndim)
    tri = jnp.asarray(np.triu(np.ones((tm, tm), np.float32), 1), BF16)
    ws = [mw["woa"], mw["wob"], mw["wout"], mw["gffn"], mw["wrh"], mw["wrl"], mw["br"], tri]
    sds = jax.ShapeDtypeStruct
    return pl.pallas_call(
        _merge_kernel,
        out_shape=[sds((N, D), F32), sds((N, D), F32), sds((2, N), I32), sds((2, N), I32), sds((N, LANES), F32),
                   sds((N_EXPERTS, 1), F32)],
        grid=(N // tm,),
        in_specs=[tok(D), tok(512), tok(512), tok(D), tok(D)] + [full(a) for a in ws],
        out_specs=[tok(D), tok(D), rows(2), rows(2), tok(LANES), pl.BlockSpec((N_EXPERTS, 1), lambda i: (0, 0))],
        scratch_shapes=[pltpu.VMEM((N_EXPERTS, 1), F32)],
        compiler_params=_cparams(("arbitrary",)), name="merge")(x, ya, yb, ga, gb, *ws)


def _merge_weights(w_oa, w_ob, w_out, g_ffn, w_group, b_group, w_expert, b_expert):
    D = w_out.shape[0]
    wr = jnp.concatenate([w_group.T, jnp.zeros((8 - N_GROUPS, D), F32), w_expert.T], axis=0)
    wrh = wr.astype(BF16)
    wrl = (wr - wrh.astype(F32)).astype(BF16)
    br = jnp.concatenate([b_group, jnp.full((8 - N_GROUPS,), NEG_INF, F32), b_expert]).reshape(ROUTER_ROWS, 1)
    return dict(woa=w_oa.astype(BF16), wob=w_ob.astype(BF16), wout=w_out.astype(BF16),
                gffn=g_ffn.reshape(1, D).astype(F32), wrh=wrh, wrl=wrl, br=br.astype(F32))


MOE_ROWS = 256


def _dispatch_kernel(seg_ref, eid_ref, rank_ref, hn_ref, xs_in_ref, xs_ref, sem):
    del xs_in_ref
    tm = hn_ref.shape[0]

    def row_copy(t, k):
        dest = seg_ref[eid_ref[k, t]] + rank_ref[k, t]
        return pltpu.make_async_copy(hn_ref.at[pl.ds(t, 1), :], xs_ref.at[pl.ds(dest, 1), :], sem)

    def start(t, c):
        row_copy(t, 0).start()
        row_copy(t, 1).start()
        return c

    def wait(t, c):
        row_copy(t, 0).wait()
        row_copy(t, 1).wait()
        return c

    lax.fori_loop(0, tm, start, 0)
    lax.fori_loop(0, tm, wait, 0)


def _dispatch(seg_start, eid, rank, hn, n_rows, tm):
    N, D = hn.shape
    smem_rows = pl.BlockSpec((2, tm), lambda i, seg: (0, i), memory_space=pltpu.SMEM)
    grid_spec = pltpu.PrefetchScalarGridSpec(
        num_scalar_prefetch=1, grid=(N // tm,),
        in_specs=[smem_rows, smem_rows, pl.BlockSpec((tm, D), lambda i, seg: (i, 0)),
                  pl.BlockSpec(memory_space=pl.ANY)],
        out_specs=pl.BlockSpec(memory_space=pl.ANY),
        scratch_shapes=[pltpu.SemaphoreType.DMA(())])
    return pl.pallas_call(
        _dispatch_kernel, out_shape=jax.ShapeDtypeStruct((n_rows, D), F32), grid_spec=grid_spec,
        input_output_aliases={4: 0},
        compiler_params=_cparams(("arbitrary",)), name="moe_dispatch",
    )(seg_start, eid, rank, hn, jnp.zeros((n_rows, D), F32))


def _expert_kernel(blk_e_ref, xs_ref, w1_ref, w3_ref, w2_ref, ys_ref):
    del blk_e_ref
    xb = xs_ref[...].astype(BF16)
    hid = jax.nn.silu(_dot(xb, w1_ref[0])) * _dot(xb, w3_ref[0])
    ys_ref[...] = _dot(hid.astype(BF16), w2_ref[0])


def _experts(blk_e, xs, w1, w3, w2):
    P, D = xs.shape
    n_blocks = P // MOE_ROWS
    grid_spec = pltpu.PrefetchScalarGridSpec(
        num_scalar_prefetch=1, grid=(n_blocks,),
        in_specs=[pl.BlockSpec((MOE_ROWS, D), lambda i, e: (i, 0)),
                  pl.BlockSpec((1, D, D_EXPERT), lambda i, e: (e[i], 0, 0)),
                  pl.BlockSpec((1, D, D_EXPERT), lambda i, e: (e[i], 0, 0)),
                  pl.BlockSpec((1, D_EXPERT, D), lambda i, e: (e[i], 0, 0))],
        out_specs=pl.BlockSpec((MOE_ROWS, D), lambda i, e: (i, 0)))
    return pl.pallas_call(
        _expert_kernel, out_shape=jax.ShapeDtypeStruct((P, D), F32), grid_spec=grid_spec,
        compiler_params=_cparams(("arbitrary",)), name="moe_experts")(blk_e, xs, w1, w3, w2)


def _combine_kernel(seg_ref, eid_ref, rank_ref, h_ref, wcol_ref, ys_ref, out_ref, g_ref, sem):
    tm = h_ref.shape[0]

    def row_copy(t, k):
        dest = seg_ref[eid_ref[k, t]] + rank_ref[k, t]
        return pltpu.make_async_copy(ys_ref.at[pl.ds(dest, 1), :], g_ref.at[k, pl.ds(t, 1), :], sem)

    def start(t, c):
        row_copy(t, 0).start()
        row_copy(t, 1).start()
        return c

    def wait(t, c):
        row_copy(t, 0).wait()
        row_copy(t, 1).wait()
        return c

    lax.fori_loop(0, tm, start, 0)
    lax.fori_loop(0, tm, wait, 0)
    w = wcol_ref[...]
    out_ref[...] = h_ref[...] + (g_ref[0] * w[:, 0:1] + g_ref[1] * w[:, 1:2])


def _combine(seg_start, eid, rank, h, wcol, ys, tm):
    N, D = h.shape
    smem_rows = pl.BlockSpec((2, tm), lambda i, seg: (0, i), memory_space=pltpu.SMEM)
    grid_spec = pltpu.PrefetchScalarGridSpec(
        num_scalar_prefetch=1, grid=(N // tm,),
        in_specs=[smem_rows, smem_rows, pl.BlockSpec((tm, D), lambda i, seg: (i, 0)),
                  pl.BlockSpec((tm, LANES), lambda i, seg: (i, 0)), pl.BlockSpec(memory_space=pl.ANY)],
        out_specs=pl.BlockSpec((tm, D), lambda i, seg: (i, 0)),
        scratch_shapes=[pltpu.VMEM((2, tm, D), F32), pltpu.SemaphoreType.DMA(())])
    return pl.pallas_call(
        _combine_kernel, out_shape=jax.ShapeDtypeStruct((N, D), F32), grid_spec=grid_spec,
        compiler_params=_cparams(("arbitrary",)), name="moe_combine")(seg_start, eid, rank, h, wcol, ys)


def _merge_and_ffn(x, ya, yb, ga, gb, mw, ew, tm):
    N, D = x.shape
    h, hn, eid, rank, wcol, counts = _merge(x, ya, yb, ga, gb, mw, tm)
    counts = counts[:, 0].astype(I32)
    padded = (counts + MOE_ROWS - 1) // MOE_ROWS * MOE_ROWS
    seg_end = jnp.cumsum(padded)
    seg_start = (seg_end - padded).astype(I32)
    n_blocks = -(-(2 * N) // MOE_ROWS) + N_EXPERTS
    block_row = jnp.arange(n_blocks, dtype=I32) * MOE_ROWS
    blk_e = jnp.minimum(jnp.sum(seg_end[None, :] <= block_row[:, None], axis=1), N_EXPERTS - 1).astype(I32)
    xs = _dispatch(seg_start, eid, rank, hn, n_blocks * MOE_ROWS, tm)
    ys = _experts(blk_e, xs, ew["w1"], ew["w3"], ew["w2"])
    return _combine(seg_start, eid, rank, h, wcol, ys, tm)


PROJ_TILE = 256


def kernel(x_prompt, x_sample, cache_a_k, cache_a_v, cache_idx_k, cache_mla_latent, cache_mla_krope, page_table,
           rel_bias, g_attn, w_in, b_gate, g_a_q, g_a_k, g_m_qa, w_uq, g_m_qn, g_m_qr, g_m_kva, g_m_kr, w_uk, g_m_kn,
           w_uv, w_oa, w_ob, w_out, g_ffn, w_group, b_group, w_expert, b_expert, w_e1, w_e3, w_e2):
    B, S, D = x_prompt.shape
    l = 0
    pw = _proj_weights(g_attn[l], w_in[l], b_gate[l], g_a_q[l], g_a_k[l], g_m_qa[l], w_uq[l], g_m_qn[l], g_m_qr[l],
                       g_m_kva[l], g_m_kr[l], w_uk[l], g_m_kn[l], w_uv[l])
    pp = _proj(x_prompt, jnp.arange(S, dtype=I32), pw, PROJ_TILE)
    ya_p = _dsa_prompt(pp, _bias_tiles(rel_bias), PROJ_TILE)
    yb_p = _mla_prompt(pp, PROJ_TILE)
    mw = _merge_weights(w_oa[l], w_ob[l], w_out[l], g_ffn[l], w_group[l], b_group[l], w_expert[l], b_expert[l])
    ew = dict(w1=w_e1[l].astype(BF16), w3=w_e3[l].astype(BF16), w2=w_e2[l].astype(BF16))
    flat = lambda a: a.reshape(B * S, a.shape[-1])
    y_p = _merge_and_ffn(flat(x_prompt), flat(ya_p), flat(yb_p), flat(pp["ga"]), flat(pp["gb"]), mw, ew, PROJ_TILE)

    DB = x_sample.shape[0]
    n_pages = page_table.shape[1]
    past = n_pages * PAGE_SIZE
    ps = _proj(x_sample.reshape(1, DB, D), jnp.full((DB,), past, I32), pw, DB)
    ps = {k: v[0] for k, v in ps.items()}
    pt_flat = page_table.reshape(-1).astype(I32)
    heads3 = lambda a: a.reshape(DB, 8, LANES)
    first_row = lambda a: jnp.pad(a[:, None, :], ((0, 0), (0, PAGE_SIZE - 1), (0, 0)))
    first_col = lambda a: jnp.pad(a[:, :, None], ((0, 0), (0, 0), (0, PAGE_SIZE - 1)))
    idx_t = jnp.swapaxes(cache_idx_k[l:l + 1], 2, 3)
    krope_t = jnp.swapaxes(cache_mla_krope[l:l + 1], 2, 3)
    ak_t = jnp.transpose(cache_a_k[l:l + 1], (0, 1, 3, 4, 2))
    av_t = jnp.transpose(cache_a_v[l:l + 1], (0, 1, 3, 4, 2))
    sc, sc_new = _idx_sample(pt_flat, heads3(ps["iqp"]), ps["iwT"], first_col(ps["ik"]), idx_t, n_pages)
    scores = jnp.concatenate([sc.reshape(DB, past), sc_new.reshape(DB, LANES)], axis=1)
    topk = min(TOPK_MAX, (past + 1) // 4)
    sel = _sel_sample(scores.T, topk).T.reshape(DB, n_pages + 1, LANES)
    slot = jnp.arange(PAGE_SIZE, dtype=I32)
    bias2 = jnp.stack([rel_bias[_rel_bucket(PAGE_SIZE - slot)].T,
                       rel_bias[_rel_bucket(2 * PAGE_SIZE - slot)].T]).astype(F32)
    ya_s = _dsa_sample(pt_flat, heads3(ps["aqp"]), sel[:, :n_pages], sel[:, n_pages:], ps["ak"].reshape(DB, 1, 256),
                       ps["av"].reshape(DB, 1, 256), bias2, rel_bias[0].reshape(A_HEADS, 1).astype(F32),
                       ak_t, av_t, n_pages).reshape(DB, A_HEADS * A_HEAD_DIM)
    qgt = (ps["qn"].astype(F32) * jnp.tile(g_m_kn[l], M_HEADS)[None, :]).T
    wukt = w_uk[l].reshape(M_KV_RANK, M_HEADS * M_NOPE).T.astype(BF16)
    yb_s = _mla_sample(pt_flat, qgt, heads3(ps["qr"]), first_row(ps["ckv"]), first_col(ps["kr"]), wukt, pw["wuv"],
                       cache_mla_latent[l:l + 1], krope_t, n_pages).reshape(DB, M_HEADS * M_V)
    y_s = _merge_and_ffn(x_sample.reshape(DB, D), ya_s.astype(BF16), yb_s.astype(BF16), ps["ga"], ps["gb"], mw, ew, DB)

    kv = lambda a, n: a.reshape(1, n, -1, A_KV_HEADS, A_HEAD_DIM)
    return (y_p.reshape(B, S, D), y_s.reshape(DB, 1, D),
            kv(pp["ak"], B), kv(pp["av"], B), pp["ik"][None], pp["ckv"][None], pp["kr"][None],
            kv(ps["ak"], DB), kv(ps["av"], DB), ps["ik"].reshape(1, DB, 1, IDX_DIM),
            ps["ckv"].reshape(1, DB, 1, M_KV_RANK), ps["kr"].reshape(1, DB, 1, M_ROPE))
```

```python
import functools
import math

import jax
import jax.numpy as jnp
import numpy as np
from jax import lax
from jax.experimental import pallas as pl
from jax.experimental.pallas import tpu as pltpu

F32 = jnp.float32
BF16 = jnp.bfloat16
I32 = jnp.int32

LANES = 128
EPS = 1e-6
A_HEADS = 8
A_KV_HEADS = 4
A_HEAD_DIM = 64
IDX_HEADS = 8
IDX_DIM = 64
TOPK_MAX = 256
INDEX_SCALE = (IDX_HEADS ** -0.5) * (IDX_DIM ** -0.5)
M_HEADS = 8
M_Q_RANK = 384
M_KV_RANK = 256
M_NOPE = 64
M_ROPE = 32
M_V = 64
MLA_SCALE = (M_NOPE + M_ROPE) ** -0.5
ROPE_BASE = 10000.0
REL_BUCKETS = 32
REL_MAX_DIST = 128
N_GROUPS = 4
EXPERTS_PER_GROUP = 8
N_EXPERTS = 32
D_EXPERT = 256
MOE_BLOCK = 128
Q_BLOCK = 128
PAGE_SIZE = 128
IN_SPLITS = (512, 256, 256, 512, 8, 64, 384, 256, 32, 1024, 1024)
NEG_INF = float("-inf")
VMEM_LIMIT = 56 * 1024 * 1024


def _dot(a, b):
    return jnp.dot(a, b, preferred_element_type=F32)


def _dot_nt(a, b):
    return lax.dot_general(a, b, (((1,), (1,)), ((), ())), preferred_element_type=F32)


def _cparams(sem):
    return pltpu.CompilerParams(dimension_semantics=sem, vmem_limit_bytes=VMEM_LIMIT)


def _expand_heads(z):
    lane = lax.broadcasted_iota(I32, (z.shape[0], LANES), 1)
    lo = lane < 64
    parts = []
    for j in range(z.shape[1] // LANES):
        c = z[:, j * LANES:(j + 1) * LANES]
        parts.append(jnp.where(lo, c, 0.0))
        parts.append(jnp.where(lo, pltpu.roll(c, 64, 1), 0.0))
    return jnp.concatenate(parts, axis=1)


def _rope_slot(r, c_tab, s1_tab, s2_tab):
    return r * c_tab + pltpu.roll(r, LANES - 16, 1) * s1_tab + pltpu.roll(r, 16, 1) * s2_tab


def _proj_kernel(x_ref, ctab_ref, s1tab_ref, s2tab_ref, gattn_ref, w_ref, bgate_ref, e64_ref,
                 gaq_ref, gak_ref, gqa_ref, wuq_ref, gqn_ref, gqr_ref, gkva_ref, gkr_ref,
                 wuk_ref, gkn_ref, wuv_ref,
                 ak_o, av_o, ik_o, ckv_o, kr_o,
                 aqp_o, akp_o, avt_o, iqp_o, iwt_o, ikb_o, qn_o, qr_o, kp_o, vmt_o, ga_o, gb_o):
    x = x_ref[0]
    ms = jnp.mean(x * x, axis=-1, keepdims=True)
    xn = x * lax.rsqrt(ms + EPS) * gattn_ref[...]
    xb = xn.astype(BF16)
    ctab, s1tab, s2tab = ctab_ref[...], s1tab_ref[...], s2tab_ref[...]

    def head_norm(z, width):
        ss = _dot((z * z).astype(BF16), e64_ref[:width, :width])
        return z * lax.rsqrt(ss * (1.0 / 64.0) + EPS)

    z = _dot(xb, w_ref[:, 0:512])
    zn = head_norm(z, 512) * gaq_ref[...] * (A_HEAD_DIM ** -0.5)
    aqp_o[0] = _expand_heads(zn).astype(BF16)
    z = _dot(xb, w_ref[:, 512:768])
    zn = head_norm(z, 256) * gak_ref[...]
    ak_o[0] = zn
    akp_o[0] = _expand_heads(zn).astype(BF16)
    z = _dot(xb, w_ref[:, 768:1024])
    av_o[0] = z
    avt_o[0, 0] = z.T.astype(BF16)
    z = _dot(xb, w_ref[:, 1024:1536])
    iqp_o[0] = _expand_heads(z).astype(BF16)
    z = _dot(xb, w_ref[:, 1536:1664]) * INDEX_SCALE
    iwt_o[0] = z.T[0:IDX_HEADS, :]
    z = _dot(xb, w_ref[:, 1664:1792])
    ik_o[0] = z[:, 0:IDX_DIM]
    ikb_o[0] = z.astype(BF16)
    z = _dot(xb, w_ref[:, 1792:2176])
    cq = z * lax.rsqrt(jnp.mean(z * z, axis=-1, keepdims=True) + EPS) * gqa_ref[...]
    cqb = cq.astype(BF16)
    z = _dot(cqb, wuq_ref[:, 0:512])
    qn_o[0] = (head_norm(z, 512) * gqn_ref[...] * MLA_SCALE).astype(BF16)
    parts = []
    for h in range(M_HEADS):
        r = _dot(cqb, wuq_ref[:, 512 + h * LANES:512 + (h + 1) * LANES])
        ss = jnp.sum(r * r, axis=-1, keepdims=True) * (1.0 / M_ROPE)
        rn = r * lax.rsqrt(ss + EPS) * gqr_ref[...]
        parts.append((_rope_slot(rn, ctab, s1tab, s2tab) * MLA_SCALE).astype(BF16))
    qr_o[0] = jnp.concatenate(parts, axis=1)
    z = _dot(xb, w_ref[:, 2176:2432])
    ckv = z * lax.rsqrt(jnp.mean(z * z, axis=-1, keepdims=True) + EPS) * gkva_ref[...]
    ckv_o[0] = ckv
    cb = ckv.astype(BF16)
    kn = _dot(cb, wuk_ref[...])
    knb = (head_norm(kn, 512) * gkn_ref[...]).astype(BF16)
    vmt_o[0, 0] = _dot(cb, wuv_ref[...]).T.astype(BF16)
    z = _dot(xb, w_ref[:, 2432:2560])
    ss = jnp.sum(z * z, axis=-1, keepdims=True) * (1.0 / M_ROPE)
    kr = _rope_slot(z * lax.rsqrt(ss + EPS) * gkr_ref[...], ctab, s1tab, s2tab)
    kr_o[0] = kr[:, 0:M_ROPE]
    krb = kr.astype(BF16)
    kp_o[0] = jnp.concatenate([p for j in range(4) for p in (knb[:, j * LANES:(j + 1) * LANES], krb)], axis=1)
    z = _dot(xb, w_ref[:, 2560:3584]) + bgate_ref[0:1, :]
    ga_o[0] = jax.nn.sigmoid(z).astype(BF16)
    z = _dot(xb, w_ref[:, 3584:4608]) + bgate_ref[1:2, :]
    gb_o[0] = jax.nn.sigmoid(z).astype(BF16)


def _proj_weights(g_attn, w_in, b_gate, g_a_q, g_a_k, g_m_qa, w_uq, g_m_qn, g_m_qr, g_m_kva, g_m_kr,
                  w_uk, g_m_kn, w_uv):
    D = w_in.shape[0]
    o = [int(v) for v in np.cumsum(IN_SPLITS)]

    def padc(a, width=LANES):
        return jnp.pad(a, ((0, 0), (0, width - a.shape[1])))

    w1 = jnp.concatenate([
        w_in[:, 0:o[3]], padc(w_in[:, o[3]:o[4]]), padc(w_in[:, o[4]:o[5]]), w_in[:, o[5]:o[7]],
        padc(w_in[:, o[7]:o[8]]), w_in[:, o[8]:o[10]]], axis=1).astype(BF16)
    wq = w_uq.reshape(M_Q_RANK, M_HEADS, M_NOPE + M_ROPE)
    wuq = jnp.concatenate(
        [wq[:, :, :M_NOPE].reshape(M_Q_RANK, M_HEADS * M_NOPE)]
        + [padc(wq[:, h, M_NOPE:]) for h in range(M_HEADS)], axis=1).astype(BF16)
    e64 = jnp.asarray(np.kron(np.eye(8, dtype=np.float32), np.ones((64, 64), np.float32)), BF16)
    row = lambda v: v.reshape(1, -1).astype(F32)
    return dict(
        gattn=row(g_attn), w1=w1, bgate=b_gate.astype(F32), e64=e64,
        gaq=row(jnp.tile(g_a_q, A_HEADS)), gak=row(jnp.tile(g_a_k, A_KV_HEADS)), gqa=row(g_m_qa), wuq=wuq,
        gqn=row(jnp.tile(g_m_qn, M_HEADS)), gqr=padc(row(g_m_qr)), gkva=row(g_m_kva), gkr=padc(row(g_m_kr)),
        wuk=w_uk.reshape(M_KV_RANK, M_HEADS * M_NOPE).astype(BF16), gkn=row(jnp.tile(g_m_kn, M_HEADS)),
        wuv=w_uv.reshape(M_KV_RANK, M_HEADS * M_V).astype(BF16))


def _rope_tables(pos):
    half = M_ROPE // 2
    inv = ROPE_BASE ** (-jnp.arange(half, dtype=F32) / half)
    ang = pos.astype(F32)[:, None] * inv
    cos, sin = jnp.cos(ang), jnp.sin(ang)
    n = pos.shape[0]
    ctab = jnp.concatenate([cos, cos, jnp.ones((n, LANES - 2 * half), F32)], axis=1)
    s1tab = jnp.concatenate([-sin, jnp.zeros((n, LANES - half), F32)], axis=1)
    s2tab = jnp.concatenate([jnp.zeros((n, half), F32), sin, jnp.zeros((n, LANES - 2 * half), F32)], axis=1)
    return ctab, s1tab, s2tab


def _proj(x, pos, pw, tm):
    B, S, D = x.shape
    ctab, s1tab, s2tab = _rope_tables(pos)
    tok = lambda w: pl.BlockSpec((1, tm, w), lambda s, b: (b, s, 0))
    tokT = lambda w: pl.BlockSpec((1, w, tm), lambda s, b: (b, 0, s))
    tokC = lambda w: pl.BlockSpec((1, 1, w, tm), lambda s, b: (b, s, 0, 0))
    tab = pl.BlockSpec((tm, LANES), lambda s, b: (s, 0))
    full = lambda a: pl.BlockSpec(a.shape, lambda s, b: (0,) * a.ndim)
    wnames = ["gattn", "w1", "bgate", "e64", "gaq", "gak", "gqa", "wuq", "gqn", "gqr", "gkva", "gkr", "wuk", "gkn",
              "wuv"]
    ws = [pw[n] for n in wnames]
    sds = lambda shape, dt: jax.ShapeDtypeStruct(shape, dt)
    out_shape = [
        sds((B, S, 256), F32), sds((B, S, 256), F32), sds((B, S, IDX_DIM), F32), sds((B, S, M_KV_RANK), F32),
        sds((B, S, M_ROPE), F32),
        sds((B, S, 1024), BF16), sds((B, S, 512), BF16), sds((B, S // tm, 256, tm), BF16), sds((B, S, 1024), BF16),
        sds((B, IDX_HEADS, S), F32), sds((B, S, LANES), BF16), sds((B, S, 512), BF16), sds((B, S, 1024), BF16),
        sds((B, S, 1024), BF16), sds((B, S // tm, 512, tm), BF16), sds((B, S, D), BF16), sds((B, S, D), BF16)]
    out_specs = [tok(256), tok(256), tok(IDX_DIM), tok(M_KV_RANK), tok(M_ROPE),
                 tok(1024), tok(512), tokC(256), tok(1024), tokT(IDX_HEADS), tok(LANES), tok(512), tok(1024),
                 tok(1024), tokC(512), tok(D), tok(D)]
    outs = pl.pallas_call(
        _proj_kernel, out_shape=out_shape, grid=(S // tm, B),
        in_specs=[tok(D), tab, tab, tab] + [full(a) for a in ws], out_specs=out_specs,
        compiler_params=_cparams(("arbitrary", "arbitrary")), name="proj")(x, ctab, s1tab, s2tab, *ws)
    names = ["ak", "av", "ik", "ckv", "kr", "aqp", "akp", "avT", "iqp", "iwT", "ikb", "qn", "qr", "kp", "vmT",
             "ga", "gb"]
    return dict(zip(names, outs))


INT_MIN = -2 ** 31


def _rel_bucket(dist):
    n = jnp.maximum(dist, 0)
    max_exact = REL_BUCKETS // 2
    nf = jnp.maximum(n, 1).astype(F32)
    large = max_exact + (jnp.log(nf / max_exact) / math.log(REL_MAX_DIST / max_exact)
                         * (REL_BUCKETS - max_exact)).astype(I32)
    large = jnp.minimum(large, REL_BUCKETS - 1)
    return jnp.where(n < max_exact, n, large)


def _bias_tiles(rel_bias):
    qq = jnp.arange(Q_BLOCK, dtype=I32)[None, :]
    kk = jnp.arange(Q_BLOCK, dtype=I32)[:, None]
    tiles = [jnp.moveaxis(rel_bias[_rel_bucket(o * Q_BLOCK + qq - kk)], -1, 0) for o in range(3)]
    return jnp.stack(tiles).astype(F32)


def _sortable_key(sc):
    bits = pltpu.bitcast(sc, I32)
    key = jnp.where(bits < 0, bits ^ jnp.int32(0x7FFFFFFF), bits)
    return jnp.where(bits == jnp.int32(INT_MIN), 0, key)


def _colsum8(m):
    return m.reshape(m.shape[0] // 8, 8, LANES).sum(axis=0)


def _select_topk_columns(key_ref, n_chunks, kb, topk, active, idx_bits):
    def count(pred):
        def body(c, acc):
            r0 = pl.multiple_of(c * kb, kb)
            k = key_ref[pl.ds(r0, kb), :]
            return acc + _colsum8(pred(k, r0).astype(I32))
        acc = lax.fori_loop(0, n_chunks, body, jnp.zeros((8, LANES), I32))
        return jnp.sum(acc, axis=0, keepdims=True)

    zero = jnp.zeros((1, LANES), I32)
    cnt0 = count(lambda k, r0: k >= zero)
    t0 = jnp.where(jnp.logical_and(cnt0 >= topk, active), 0, INT_MIN).astype(I32)

    def bit_body(j, t):
        cand = t | jnp.left_shift(jnp.int32(1), 30 - j)
        cnt = count(lambda k, r0: k >= cand)
        return jnp.where(cnt >= topk, cand, t)

    t = lax.fori_loop(0, jnp.where(active, 31, 0), bit_body, t0)
    need = topk - count(lambda k, r0: k > t)
    excess = jnp.max(count(lambda k, r0: k == t) - need) > 0
    tie_search = jnp.logical_and(active, excess)

    def tie_body(j, cut):
        cand = cut | jnp.left_shift(jnp.int32(1), idx_bits - 1 - j)

        def pred(k, r0):
            row = r0 + lax.broadcasted_iota(I32, (kb, LANES), 0)
            return jnp.logical_and(k == t, row < cand)
        return jnp.where(count(pred) <= need, cand, cut)

    cut0 = jnp.where(tie_search, zero, jnp.int32(2 ** idx_bits - 1))
    cut = lax.fori_loop(0, jnp.where(tie_search, idx_bits, 0), tie_body, cut0)
    return t, cut


def _softmax_step(s, m_old, l_old):
    m_new = jnp.maximum(m_old, jnp.max(s, axis=0, keepdims=True))
    m_safe = jnp.where(m_new == NEG_INF, 0.0, m_new)
    p = jnp.exp(s - m_safe)
    alpha = jnp.exp(m_old - m_safe)
    l_new = alpha * l_old + jnp.sum(p, axis=0, keepdims=True)
    return p, alpha, m_new, l_new


def _dsa_prompt_kernel(aq_ref, iq_ref, iw_ref, ak_ref, ik_ref, avt_ref, bias_ref, y_ref,
                       sc_ref, key_ref, acc_ref, m_ref, l_ref, *, topk, kb, idx_bits):
    i = pl.program_id(1)
    n_keys = (i + 1) * Q_BLOCK
    n_chunks = (n_keys + kb - 1) // kb
    q_pos = i * Q_BLOCK + lax.broadcasted_iota(I32, (kb, LANES), 1)
    row_iota = lax.broadcasted_iota(I32, (kb, LANES), 0)

    iq = iq_ref[0]
    half = IDX_HEADS // 2
    iq_stacks = [jnp.concatenate([iq[:, h * LANES:(h + 1) * LANES] for h in range(u * half, (u + 1) * half)], axis=0)
                 for u in range(2)]
    iw = iw_ref[0]

    def score_body(c, carry):
        r0 = pl.multiple_of(c * kb, kb)
        ik = ik_ref[0, pl.ds(r0, kb), :]
        d = [_dot_nt(ik, iq_stacks[u]) for u in range(2)]
        sc = None
        for h in range(IDX_HEADS):
            term = iw[h:h + 1, :] * jnp.maximum(d[h // half][:, (h % half) * LANES:(h % half + 1) * LANES], 0.0)
            sc = term if sc is None else sc + term
        sc = jnp.where(r0 + row_iota <= q_pos, sc, NEG_INF)
        key_ref[pl.ds(r0, kb), :] = _sortable_key(sc)
        return carry

    lax.fori_loop(0, n_chunks, score_body, 0)
    t, cut = _select_topk_columns(key_ref, n_chunks, kb, topk, n_keys > topk, idx_bits)

    def mask_body(c, carry):
        r0 = pl.multiple_of(c * kb, kb)
        k = key_ref[pl.ds(r0, kb), :]
        row = r0 + row_iota
        sel = jnp.logical_or(k > t, jnp.logical_and(k == t, row < cut))
        sc_ref[pl.ds(r0, kb), :] = jnp.where(jnp.logical_and(sel, row <= q_pos), 0.0, NEG_INF)
        return carry

    lax.fori_loop(0, n_chunks, mask_body, 0)

    aq = aq_ref[0]
    aq_stack = [jnp.concatenate([aq[:, (2 * g) * LANES:(2 * g + 1) * LANES],
                                 aq[:, (2 * g + 1) * LANES:(2 * g + 2) * LANES]], axis=0)
                for g in range(A_KV_HEADS)]
    m_ref[...] = jnp.full(m_ref.shape, NEG_INF, F32)
    l_ref[...] = jnp.zeros(l_ref.shape, F32)
    acc_ref[...] = jnp.zeros(acc_ref.shape, F32)
    n_sub = kb // Q_BLOCK

    def scores(c):
        r0 = pl.multiple_of(c * kb, kb)
        return [_dot_nt(ak_ref[0, pl.ds(r0, kb), g * LANES:(g + 1) * LANES], aq_stack[g])
                for g in range(A_KV_HEADS)]

    def attn_body(c, s_cur):
        s_next = scores(jnp.minimum(c + 1, n_chunks - 1))
        r0 = pl.multiple_of(c * kb, kb)
        addm = sc_ref[pl.ds(r0, kb), :]
        tile_idx = [jnp.clip(i - (c * n_sub + u), 0, 2) for u in range(n_sub)]

        def softmax_and_pv(g, s_pair):
            ps, alphas = [], []
            for hh in range(2):
                h = 2 * g + hh
                bias = jnp.concatenate([bias_ref[tile_idx[u], h] for u in range(n_sub)], axis=0)
                s = s_pair[:, hh * LANES:(hh + 1) * LANES] + bias + addm
                p, alpha, m_new, l_new = _softmax_step(s, m_ref[h:h + 1, :], l_ref[h:h + 1, :])
                m_ref[h:h + 1, :] = m_new
                l_ref[h:h + 1, :] = l_new
                ps.append(p.astype(BF16))
                alphas.append(alpha)
            o_pair = _dot(avt_ref[0, c, g * A_HEAD_DIM:(g + 1) * A_HEAD_DIM, :], jnp.concatenate(ps, axis=1))
            for hh in range(2):
                h = 2 * g + hh
                rows = slice(h * A_HEAD_DIM, (h + 1) * A_HEAD_DIM)
                acc_ref[rows, :] = alphas[hh] * acc_ref[rows, :] + o_pair[:, hh * LANES:(hh + 1) * LANES]

        for g in range(A_KV_HEADS):
            softmax_and_pv(g, s_cur[g])
        return s_next

    lax.fori_loop(0, n_chunks, attn_body, scores(0))
    outs = [acc_ref[h * A_HEAD_DIM:(h + 1) * A_HEAD_DIM, :] / l_ref[h:h + 1, :] for h in range(A_HEADS)]
    y_ref[0] = jnp.concatenate(outs, axis=0).T.astype(BF16)


def _dsa_prompt(p, bias_tiles, kb):
    B, S, _ = p["aqp"].shape
    topk = min(TOPK_MAX, S // 4)
    nqb = S // Q_BLOCK
    idx_bits = int(S).bit_length()
    qblk = lambda w: pl.BlockSpec((1, Q_BLOCK, w), lambda b, i: (b, i, 0))
    kern = functools.partial(_dsa_prompt_kernel, topk=topk, kb=kb, idx_bits=idx_bits)
    return pl.pallas_call(
        kern, out_shape=jax.ShapeDtypeStruct((B, S, A_HEADS * A_HEAD_DIM), BF16), grid=(B, nqb),
        in_specs=[qblk(1024), qblk(1024), pl.BlockSpec((1, IDX_HEADS, Q_BLOCK), lambda b, i: (b, 0, i)),
                  pl.BlockSpec((1, S, 512), lambda b, i: (b, 0, 0)),
                  pl.BlockSpec((1, S, LANES), lambda b, i: (b, 0, 0)),
                  pl.BlockSpec((1, S // kb, 256, kb), lambda b, i: (b, 0, 0, 0)),
                  pl.BlockSpec(bias_tiles.shape, lambda b, i: (0, 0, 0, 0))],
        out_specs=qblk(A_HEADS * A_HEAD_DIM),
        scratch_shapes=[pltpu.VMEM((S, LANES), F32), pltpu.VMEM((S, LANES), I32),
                        pltpu.VMEM((A_HEADS * A_HEAD_DIM, LANES), F32), pltpu.VMEM((A_HEADS, LANES), F32),
                        pltpu.VMEM((A_HEADS, LANES), F32)],
        compiler_params=_cparams(("arbitrary", "arbitrary")), name="dsa_prompt",
    )(p["aqp"], p["iqp"], p["iwT"], p["akp"], p["ikb"], p["avT"], bias_tiles)


def _causal_tiles():
    qq = np.arange(Q_BLOCK)[None, :]
    kk = np.arange(Q_BLOCK)[:, None]
    tri = np.where(kk <= qq, 0.0, -np.inf).astype(np.float32)
    return jnp.asarray(np.stack([np.zeros_like(tri), tri, np.full_like(tri, -np.inf)]))


def _mla_prompt_kernel(qn_ref, qr_ref, kp_ref, vmt_ref, mask_ref, y_ref, acc_ref, m_ref, l_ref, *, kb):
    i = pl.program_id(1)
    n_chunks = ((i + 1) * Q_BLOCK + kb - 1) // kb
    n_sub = kb // Q_BLOCK
    lo = lax.broadcasted_iota(I32, (Q_BLOCK, LANES), 1) < M_NOPE
    qn, qr = qn_ref[0], qr_ref[0]
    zero = jnp.zeros((Q_BLOCK, LANES), BF16)
    q_stack = []
    for j in range(M_HEADS // 2):
        c = qn[:, j * LANES:(j + 1) * LANES]
        top = jnp.concatenate([jnp.where(lo, c, zero), qr[:, (2 * j) * LANES:(2 * j + 1) * LANES]], axis=1)
        bot = jnp.concatenate([jnp.where(lo, zero, c), qr[:, (2 * j + 1) * LANES:(2 * j + 2) * LANES]], axis=1)
        q_stack.append(jnp.concatenate([top, bot], axis=0))
    m_ref[...] = jnp.full(m_ref.shape, NEG_INF, F32)
    l_ref[...] = jnp.zeros(l_ref.shape, F32)
    acc_ref[...] = jnp.zeros(acc_ref.shape, F32)

    def scores(c):
        r0 = pl.multiple_of(c * kb, kb)
        return [_dot_nt(kp_ref[0, pl.ds(r0, kb), j * 2 * LANES:(j + 1) * 2 * LANES], q_stack[j])
                for j in range(M_HEADS // 2)]

    def attn_body(c, s_cur):
        s_next = scores(jnp.minimum(c + 1, n_chunks - 1))
        addm = jnp.concatenate([mask_ref[jnp.clip(c * n_sub + u - i + 1, 0, 2)] for u in range(n_sub)], axis=0)
        for h in range(M_HEADS):
            s = s_cur[h // 2][:, (h % 2) * LANES:(h % 2 + 1) * LANES] + addm
            p, alpha, m_new, l_new = _softmax_step(s, m_ref[h:h + 1, :], l_ref[h:h + 1, :])
            m_ref[h:h + 1, :] = m_new
            l_ref[h:h + 1, :] = l_new
            rows = slice(h * M_V, (h + 1) * M_V)
            o = _dot(vmt_ref[0, c, rows, :], p.astype(BF16))
            acc_ref[rows, :] = alpha * acc_ref[rows, :] + o
        return s_next

    lax.fori_loop(0, n_chunks, attn_body, scores(0))
    outs = [acc_ref[h * M_V:(h + 1) * M_V, :] / l_ref[h:h + 1, :] for h in range(M_HEADS)]
    y_ref[0] = jnp.concatenate(outs, axis=0).T.astype(BF16)


def _mla_prompt(p, kb):
    B, S, _ = p["qn"].shape
    qblk = lambda w: pl.BlockSpec((1, Q_BLOCK, w), lambda b, i: (b, i, 0))
    masks = _causal_tiles()
    return pl.pallas_call(
        functools.partial(_mla_prompt_kernel, kb=kb),
        out_shape=jax.ShapeDtypeStruct((B, S, M_HEADS * M_V), BF16), grid=(B, S // Q_BLOCK),
        in_specs=[qblk(512), qblk(1024), pl.BlockSpec((1, S, 1024), lambda b, i: (b, 0, 0)),
                  pl.BlockSpec((1, S // kb, 512, kb), lambda b, i: (b, 0, 0, 0)),
                  pl.BlockSpec(masks.shape, lambda b, i: (0, 0, 0))],
        out_specs=qblk(M_HEADS * M_V),
        scratch_shapes=[pltpu.VMEM((M_HEADS * M_V, LANES), F32), pltpu.VMEM((M_HEADS, LANES), F32),
                        pltpu.VMEM((M_HEADS, LANES), F32)],
        compiler_params=_cparams(("arbitrary", "arbitrary")), name="mla_prompt",
    )(p["qn"], p["qr"], p["kp"], p["vmT"], masks)


PAGES_PER_STEP = 16


def _page_specs(shape_tail, n_pages, per_step=PAGES_PER_STEP):
    def spec(u):
        def index_map(b, j, pt):
            return (0, pt[b * n_pages + j * per_step + u]) + (0,) * len(shape_tail)
        return pl.BlockSpec((1, 1) + tuple(shape_tail), index_map)
    return [spec(u) for u in range(per_step)]


def _lane_column(mat, b):
    lane = lax.broadcasted_iota(I32, mat.shape, 1)
    return jnp.sum(jnp.where(lane == b, mat, 0.0), axis=1, keepdims=True)


IDX_PAGES_PER_STEP = 32


def _idx_sample_kernel(pt_ref, iq_ref, iwt_ref, iknew_ref, *rest):
    del pt_ref
    pages = rest[:IDX_PAGES_PER_STEP]
    sc_o, scnew_o = rest[IDX_PAGES_PER_STEP:]
    b, j = pl.program_id(0), pl.program_id(1)
    iq = iq_ref[0][:, 0:IDX_DIM]
    wcol = _lane_column(iwt_ref[...], b)

    def scores(keys_t):
        d = _dot(iq, keys_t.astype(BF16))
        return jnp.sum(wcol * jnp.maximum(d, 0.0), axis=0, keepdims=True)

    sc_o[0] = scores(jnp.concatenate([pages[u][0, 0] for u in range(IDX_PAGES_PER_STEP)], axis=1))

    @pl.when(j == 0)
    def _():
        lane = lax.broadcasted_iota(I32, (1, LANES), 1)
        scnew_o[0] = jnp.where(lane == 0, scores(iknew_ref[0]), NEG_INF)


def _idx_sample(pt_flat, iq3, iwT, iknew_page_t, cache_idx_t, n_pages):
    DB = iq3.shape[0]
    per_step = IDX_PAGES_PER_STEP
    grid_spec = pltpu.PrefetchScalarGridSpec(
        num_scalar_prefetch=1, grid=(DB, n_pages // per_step),
        in_specs=[pl.BlockSpec((1, IDX_HEADS, LANES), lambda b, j, pt: (b, 0, 0)),
                  pl.BlockSpec(iwT.shape, lambda b, j, pt: (0, 0)),
                  pl.BlockSpec((1, IDX_DIM, PAGE_SIZE), lambda b, j, pt: (b, 0, 0))]
        + _page_specs((IDX_DIM, PAGE_SIZE), n_pages, per_step),
        out_specs=[pl.BlockSpec((1, 1, per_step * PAGE_SIZE), lambda b, j, pt: (b, 0, j)),
                   pl.BlockSpec((1, 1, LANES), lambda b, j, pt: (b, 0, 0))])
    return pl.pallas_call(
        _idx_sample_kernel,
        out_shape=[jax.ShapeDtypeStruct((DB, 1, n_pages * PAGE_SIZE), F32), jax.ShapeDtypeStruct((DB, 1, LANES), F32)],
        grid_spec=grid_spec, compiler_params=_cparams(("arbitrary", "arbitrary")), name="idx_sample",
    )(pt_flat, iq3, iwT, iknew_page_t, *([cache_idx_t] * per_step))


def _sel_sample_kernel(sc_ref, sel_ref, key_ref, *, topk, kb, idx_bits):
    n_rows = sc_ref.shape[0]
    n_chunks = n_rows // kb
    row_iota = lax.broadcasted_iota(I32, (kb, LANES), 0)

    def key_body(c, carry):
        r0 = pl.multiple_of(c * kb, kb)
        key_ref[pl.ds(r0, kb), :] = _sortable_key(sc_ref[pl.ds(r0, kb), :])
        return carry

    lax.fori_loop(0, n_chunks, key_body, 0)
    t, cut = _select_topk_columns(key_ref, n_chunks, kb, topk, jnp.bool_(True), idx_bits)

    def mask_body(c, carry):
        r0 = pl.multiple_of(c * kb, kb)
        k = key_ref[pl.ds(r0, kb), :]
        sel = jnp.logical_or(k > t, jnp.logical_and(k == t, r0 + row_iota < cut))
        sel_ref[pl.ds(r0, kb), :] = sel.astype(F32)
        return carry

    lax.fori_loop(0, n_chunks, mask_body, 0)


def _sel_sample(scores_t, topk):
    L, DB = scores_t.shape
    kern = functools.partial(_sel_sample_kernel, topk=topk, kb=PAGE_SIZE, idx_bits=int(L).bit_length())
    return pl.pallas_call(
        kern, out_shape=jax.ShapeDtypeStruct((L, DB), F32), grid=(1,),
        in_specs=[pl.BlockSpec((L, DB), lambda i: (0, 0))], out_specs=pl.BlockSpec((L, DB), lambda i: (0, 0)),
        scratch_shapes=[pltpu.VMEM((L, DB), I32)],
        compiler_params=_cparams(("arbitrary",)), name="sel_sample")(scores_t)


def _dsa_sample_kernel(pt_ref, aq_ref, sel_ref, selnew_ref, knew_ref, vnew_ref, bias_ref, bnew_ref, *rest, n_steps):
    del pt_ref
    pps = PAGES_PER_STEP
    k_pages, v_pages = rest[:pps], rest[pps:2 * pps]
    y_o, acc_ref, m_ref, l_ref = rest[2 * pps:]
    j = pl.program_id(1)
    kv_of_row = lax.broadcasted_iota(I32, (A_HEADS, LANES), 0) // (A_HEADS // A_KV_HEADS)
    aq = aq_ref[0]
    q64 = aq[:, 0:A_HEAD_DIM]

    def per_kv_head(pieces):
        out = pieces[A_KV_HEADS - 1]
        for g in range(A_KV_HEADS - 2, -1, -1):
            out = jnp.where(kv_of_row[:, 0:1] == g, pieces[g], out)
        return out

    @pl.when(j == 0)
    def _():
        kn = knew_ref[0].astype(BF16).astype(F32)
        vn = vnew_ref[0].astype(BF16).astype(F32)
        lane = lax.broadcasted_iota(I32, (1, LANES), 1)
        picked = jnp.sum(jnp.where(lane == 0, selnew_ref[0], 0.0), axis=1, keepdims=True) > 0.0
        qf = aq.astype(F32)
        halves = [kn[:, 0:LANES], pltpu.roll(kn[:, 0:LANES], 64, 1), kn[:, LANES:], pltpu.roll(kn[:, LANES:], 64, 1)]
        lo = lax.broadcasted_iota(I32, (A_HEADS, LANES), 1) < A_HEAD_DIM
        s_new = jnp.sum(jnp.where(lo, qf * per_kv_head(halves), 0.0), axis=1, keepdims=True) + bnew_ref[...]
        vh = [vn[:, 0:LANES], pltpu.roll(vn[:, 0:LANES], 64, 1), vn[:, LANES:], pltpu.roll(vn[:, LANES:], 64, 1)]
        m_ref[...] = jnp.where(picked, s_new, NEG_INF)
        l_ref[...] = jnp.where(picked, 1.0, 0.0) + jnp.zeros(l_ref.shape, F32)
        acc_ref[...] = jnp.where(picked, per_kv_head(vh)[:, 0:A_HEAD_DIM] + jnp.zeros(acc_ref.shape, F32), 0.0)

    kcat = [jnp.concatenate([k_pages[u][0, 0, g] for u in range(pps)], axis=1).astype(BF16) for g in range(A_KV_HEADS)]
    s = per_kv_head([_dot(q64, kcat[g]) for g in range(A_KV_HEADS)])
    last = j == n_steps - 1
    bias = jnp.concatenate([bias_ref[1]] * (pps - 1) + [bias_ref[jnp.where(last, 0, 1)]], axis=1)
    keep = jnp.concatenate([sel_ref[0, u:u + 1, :] for u in range(pps)], axis=1) > 0.0
    s = jnp.where(keep, s + bias, NEG_INF)
    m_old = m_ref[...]
    m_new = jnp.maximum(m_old, jnp.max(s, axis=1, keepdims=True))
    m_safe = jnp.where(m_new == NEG_INF, 0.0, m_new)
    p = jnp.exp(s - m_safe)
    alpha = jnp.exp(m_old - m_safe)
    l_ref[...] = alpha * l_ref[...] + jnp.sum(p, axis=1, keepdims=True)
    m_ref[...] = m_new
    pb = p.astype(BF16)
    vcat = [jnp.concatenate([v_pages[u][0, 0, g] for u in range(pps)], axis=1).astype(BF16) for g in range(A_KV_HEADS)]
    acc_ref[...] = alpha * acc_ref[...] + per_kv_head([_dot_nt(pb, vcat[g]) for g in range(A_KV_HEADS)])

    @pl.when(last)
    def _():
        y_o[0] = acc_ref[...] / l_ref[...]


def _dsa_sample(pt_flat, aq3, sel_past, sel_new, knew, vnew, bias2, bias_new, ck_t, cv_t, n_pages):
    DB = aq3.shape[0]
    n_steps = n_pages // PAGES_PER_STEP
    page_tail = (A_KV_HEADS, A_HEAD_DIM, PAGE_SIZE)
    grid_spec = pltpu.PrefetchScalarGridSpec(
        num_scalar_prefetch=1, grid=(DB, n_steps),
        in_specs=[pl.BlockSpec((1, A_HEADS, LANES), lambda b, j, pt: (b, 0, 0)),
                  pl.BlockSpec((1, PAGES_PER_STEP, LANES), lambda b, j, pt: (b, j, 0)),
                  pl.BlockSpec((1, 1, LANES), lambda b, j, pt: (b, 0, 0)),
                  pl.BlockSpec((1, 1, 256), lambda b, j, pt: (b, 0, 0)),
                  pl.BlockSpec((1, 1, 256), lambda b, j, pt: (b, 0, 0)),
                  pl.BlockSpec(bias2.shape, lambda b, j, pt: (0, 0, 0)),
                  pl.BlockSpec(bias_new.shape, lambda b, j, pt: (0, 0))]
        + _page_specs(page_tail, n_pages) + _page_specs(page_tail, n_pages),
        out_specs=pl.BlockSpec((1, A_HEADS, A_HEAD_DIM), lambda b, j, pt: (b, 0, 0)),
        scratch_shapes=[pltpu.VMEM((A_HEADS, A_HEAD_DIM), F32), pltpu.VMEM((A_HEADS, 1), F32),
                        pltpu.VMEM((A_HEADS, 1), F32)])
    return pl.pallas_call(
        functools.partial(_dsa_sample_kernel, n_steps=n_steps),
        out_shape=jax.ShapeDtypeStruct((DB, A_HEADS, A_HEAD_DIM), F32), grid_spec=grid_spec,
        compiler_params=_cparams(("arbitrary", "arbitrary")), name="dsa_sample",
    )(pt_flat, aq3, sel_past, sel_new, knew, vnew, bias2, bias_new,
      *([ck_t] * PAGES_PER_STEP), *([cv_t] * PAGES_PER_STEP))


def _mla_sample_kernel(pt_ref, qgt_ref, qr_ref, latnew_ref, krnew_ref, wukt_ref, wuv_ref, *rest,
                       n_steps):
    del pt_ref
    lat_pages = rest[:PAGES_PER_STEP]
    kr_pages = rest[PAGES_PER_STEP:2 * PAGES_PER_STEP]
    y_o, qcol_ref, acc_ref, m_ref, l_ref = rest[2 * PAGES_PER_STEP:]
    b, j = pl.program_id(0), pl.program_id(1)
    qr = qr_ref[0][:, 0:M_ROPE]

    def attend(lat, kr_t, valid):
        nk = lat.shape[0]
        latb = lat.astype(BF16)
        qcol = qcol_ref[...]
        step = min(nk, 2 * LANES)

        def expand(c):
            return _dot_nt(wukt_ref[...], latb[c * step:(c + 1) * step, :])

        def nope_scores(knt):
            rows = []
            for h in range(M_HEADS):
                kh = knt[h * M_NOPE:(h + 1) * M_NOPE, :]
                qh = qcol[h * M_NOPE:(h + 1) * M_NOPE, :]
                if step > LANES:
                    qh = jnp.concatenate([qh] * (step // LANES), axis=1)
                ss = jnp.sum(kh * kh, axis=0, keepdims=True) * (1.0 / M_NOPE)
                raw = jnp.sum(kh * qh, axis=0, keepdims=True)
                rows.append(raw * lax.rsqrt(ss + EPS))
            return jnp.concatenate(rows, axis=0)

        parts = []
        knt_next = expand(0)
        for c in range(nk // step):
            knt_cur = knt_next
            if c + 1 < nk // step:
                knt_next = expand(c + 1)
            parts.append(nope_scores(knt_cur))
        s = jnp.concatenate(parts, axis=1) + _dot(qr, kr_t.astype(BF16))
        if valid is not None:
            s = jnp.where(valid, s, NEG_INF)
        m_old = m_ref[...]
        m_new = jnp.maximum(m_old, jnp.max(s, axis=1, keepdims=True))
        m_safe = jnp.where(m_new == NEG_INF, 0.0, m_new)
        p = jnp.exp(s - m_safe)
        alpha = jnp.exp(m_old - m_safe)
        l_ref[...] = alpha * l_ref[...] + jnp.sum(p, axis=1, keepdims=True)
        m_ref[...] = m_new
        acc_ref[...] = alpha * acc_ref[...] + _dot(p.astype(BF16), latb)

    @pl.when(j == 0)
    def _():
        qcol_ref[...] = jnp.zeros(qcol_ref.shape, F32) + _lane_column(qgt_ref[...], b)
        m_ref[...] = jnp.full(m_ref.shape, NEG_INF, F32)
        l_ref[...] = jnp.zeros(l_ref.shape, F32)
        acc_ref[...] = jnp.zeros(acc_ref.shape, F32)
        lane = lax.broadcasted_iota(I32, (M_HEADS, LANES), 1)
        attend(latnew_ref[0], krnew_ref[0], lane == 0)

    attend(jnp.concatenate([lat_pages[u][0, 0] for u in range(PAGES_PER_STEP)], axis=0),
           jnp.concatenate([kr_pages[u][0, 0] for u in range(PAGES_PER_STEP)], axis=1), None)

    @pl.when(j == n_steps - 1)
    def _():
        lat_out = (acc_ref[...] / l_ref[...]).astype(BF16)
        full = _dot(lat_out, wuv_ref[...])
        head = lax.broadcasted_iota(I32, full.shape, 1) // M_V
        row = lax.broadcasted_iota(I32, full.shape, 0)
        y_o[0] = jnp.sum(jnp.where(head == row, full, 0.0), axis=0, keepdims=True)


def _mla_sample(pt_flat, qgt, qr3, latnew, krnew, wukt, wuv, cache_lat, cache_kr, n_pages):
    DB = qr3.shape[0]
    n_steps = n_pages // PAGES_PER_STEP
    grid_spec = pltpu.PrefetchScalarGridSpec(
        num_scalar_prefetch=1, grid=(DB, n_steps),
        in_specs=[pl.BlockSpec(qgt.shape, lambda b, j, pt: (0, 0)),
                  pl.BlockSpec((1, M_HEADS, LANES), lambda b, j, pt: (b, 0, 0)),
                  pl.BlockSpec((1, PAGE_SIZE, M_KV_RANK), lambda b, j, pt: (b, 0, 0)),
                  pl.BlockSpec((1, M_ROPE, PAGE_SIZE), lambda b, j, pt: (b, 0, 0)),
                  pl.BlockSpec(wukt.shape, lambda b, j, pt: (0, 0)),
                  pl.BlockSpec(wuv.shape, lambda b, j, pt: (0, 0))]
        + _page_specs((PAGE_SIZE, M_KV_RANK), n_pages) + _page_specs((M_ROPE, PAGE_SIZE), n_pages),
        out_specs=pl.BlockSpec((1, 1, M_HEADS * M_V), lambda b, j, pt: (b, 0, 0)),
        scratch_shapes=[pltpu.VMEM((M_HEADS * M_NOPE, LANES), F32), pltpu.VMEM((M_HEADS, M_KV_RANK), F32),
                        pltpu.VMEM((M_HEADS, 1), F32), pltpu.VMEM((M_HEADS, 1), F32)])
    return pl.pallas_call(
        functools.partial(_mla_sample_kernel, n_steps=n_steps),
        out_shape=jax.ShapeDtypeStruct((DB, 1, M_HEADS * M_V), F32), grid_spec=grid_spec,
        compiler_params=_cparams(("arbitrary", "arbitrary")), name="mla_sample",
    )(pt_flat, qgt, qr3, latnew, krnew, wukt, wuv, *([cache_lat] * PAGES_PER_STEP), *([cache_kr] * PAGES_PER_STEP))


ROUTER_ROWS = 40


def _merge_kernel(x_ref, ya_ref, yb_ref, ga_ref, gb_ref, woa_ref, wob_ref, wout_ref, gffn_ref,
                  wrh_ref, wrl_ref, br_ref, tri_ref,
                  h_o, hn_o, eid_o, rank_o, wcol_o, cnt_o, cnt_ref):
    tm = x_ref.shape[0]

    @pl.when(pl.program_id(0) == 0)
    def _():
        cnt_ref[...] = jnp.zeros(cnt_ref.shape, F32)

    merged = (ga_ref[...].astype(F32) * _dot(ya_ref[...], woa_ref[...])
              + gb_ref[...].astype(F32) * _dot(yb_ref[...], wob_ref[...]))
    h = x_ref[...] + _dot(merged.astype(BF16), wout_ref[...])
    h_o[...] = h
    hn = h * lax.rsqrt(jnp.mean(h * h, axis=-1, keepdims=True) + EPS) * gffn_ref[...]
    hn_o[...] = hn
    hh = hn.astype(BF16)
    hl = (hn - hh.astype(F32)).astype(BF16)
    logit = (_dot_nt(wrh_ref[...], hh) + _dot_nt(wrh_ref[...], hl) + _dot_nt(wrl_ref[...], hh)) + br_ref[...]
    row8 = lax.broadcasted_iota(I32, (8, tm), 0)
    gl = logit[0:8, :]
    gmax = jnp.max(gl, axis=0, keepdims=True)
    gtop = jnp.min(jnp.where(gl == gmax, row8, 8), axis=0, keepdims=True)
    pg = 1.0 / jnp.sum(jnp.exp(gl - gmax), axis=0, keepdims=True)
    elg = logit[8:16, :]
    for g in range(1, N_GROUPS):
        elg = jnp.where(gtop == g, logit[8 + 8 * g:16 + 8 * g, :], elg)
    m1 = jnp.max(elg, axis=0, keepdims=True)
    i1 = jnp.min(jnp.where(elg == m1, row8, 8), axis=0, keepdims=True)
    elg2 = jnp.where(row8 == i1, NEG_INF, elg)
    m2 = jnp.max(elg2, axis=0, keepdims=True)
    i2 = jnp.min(jnp.where(elg2 == m2, row8, 8), axis=0, keepdims=True)
    e2 = jnp.exp(m2 - m1)
    den = 1.0 + e2
    w0 = (1.0 / den) * pg
    w1 = (e2 / den) * pg
    eid0 = gtop * EXPERTS_PER_GROUP + i1
    eid1 = gtop * EXPERTS_PER_GROUP + i2
    erow = lax.broadcasted_iota(I32, (N_EXPERTS, tm), 0)
    oh0 = erow == eid0
    oh1 = erow == eid1
    ohs = oh0.astype(F32) + oh1.astype(F32)
    base = cnt_ref[...] + _dot(ohs.astype(BF16), tri_ref[...])
    rank0 = jnp.sum(jnp.where(oh0, base, 0.0), axis=0, keepdims=True)
    rank1 = jnp.sum(jnp.where(oh1, base, 0.0), axis=0, keepdims=True)
    cnt_ref[...] = cnt_ref[...] + jnp.sum(ohs, axis=1, keepdims=True)
    cnt_o[...] = cnt_ref[...]
    eid_o[...] = jnp.concatenate([eid0, eid1], axis=0)
    rank_o[...] = jnp.concatenate([rank0, rank1], axis=0).astype(I32)
    wcol_o[...] = jnp.concatenate([w0, w1, jnp.zeros((LANES - 2, tm), F32)], axis=0).T


def _merge(x, ya, yb, ga, gb, mw, tm):
    N, D = x.shape
    tok = lambda w: pl.BlockSpec((tm, w), lambda i: (i, 0))
    rows = lambda r: pl.BlockSpec((r, tm), lambda i: (0, i))
    full = lambda a: pl.BlockSpec(a.shape, lambda i: (0,) * a.ndim)
    tri = jnp.asarray(np.triu(np.ones((tm, tm), np.float32), 1), BF16)
    ws = [mw["woa"], mw["wob"], mw["wout"], mw["gffn"], mw["wrh"], mw["wrl"], mw["br"], tri]
    sds = jax.ShapeDtypeStruct
    return pl.pallas_call(
        _merge_kernel,
        out_shape=[sds((N, D), F32), sds((N, D), F32), sds((2, N), I32), sds((2, N), I32), sds((N, LANES), F32),
                   sds((N_EXPERTS, 1), F32)],
        grid=(N // tm,),
        in_specs=[tok(D), tok(512), tok(512), tok(D), tok(D)] + [full(a) for a in ws],
        out_specs=[tok(D), tok(D), rows(2), rows(2), tok(LANES), pl.BlockSpec((N_EXPERTS, 1), lambda i: (0, 0))],
        scratch_shapes=[pltpu.VMEM((N_EXPERTS, 1), F32)],
        compiler_params=_cparams(("arbitrary",)), name="merge")(x, ya, yb, ga, gb, *ws)


def _merge_weights(w_oa, w_ob, w_out, g_ffn, w_group, b_group, w_expert, b_expert):
    D = w_out.shape[0]
    wr = jnp.concatenate([w_group.T, jnp.zeros((8 - N_GROUPS, D), F32), w_expert.T], axis=0)
    wrh = wr.astype(BF16)
    wrl = (wr - wrh.astype(F32)).astype(BF16)
    br = jnp.concatenate([b_group, jnp.full((8 - N_GROUPS,), NEG_INF, F32), b_expert]).reshape(ROUTER_ROWS, 1)
    return dict(woa=w_oa.astype(BF16), wob=w_ob.astype(BF16), wout=w_out.astype(BF16),
                gffn=g_ffn.reshape(1, D).astype(F32), wrh=wrh, wrl=wrl, br=br.astype(F32))


MOE_ROWS = 256


def _dispatch_kernel(seg_ref, eid_ref, rank_ref, hn_ref, xs_in_ref, xs_ref, sem):
    del xs_in_ref
    tm = hn_ref.shape[0]

    def row_copy(t, k):
        dest = seg_ref[eid_ref[k, t]] + rank_ref[k, t]
        return pltpu.make_async_copy(hn_ref.at[pl.ds(t, 1), :], xs_ref.at[pl.ds(dest, 1), :], sem)

    def start(t, c):
        row_copy(t, 0).start()
        row_copy(t, 1).start()
        return c

    def wait(t, c):
        row_copy(t, 0).wait()
        row_copy(t, 1).wait()
        return c

    lax.fori_loop(0, tm, start, 0, unroll=8)
    lax.fori_loop(0, tm, wait, 0, unroll=8)


def _dispatch(seg_start, eid, rank, hn, n_rows, tm):
    N, D = hn.shape
    smem_rows = pl.BlockSpec((2, tm), lambda i, seg: (0, i), memory_space=pltpu.SMEM)
    grid_spec = pltpu.PrefetchScalarGridSpec(
        num_scalar_prefetch=1, grid=(N // tm,),
        in_specs=[smem_rows, smem_rows, pl.BlockSpec((tm, D), lambda i, seg: (i, 0)),
                  pl.BlockSpec(memory_space=pl.ANY)],
        out_specs=pl.BlockSpec(memory_space=pl.ANY),
        scratch_shapes=[pltpu.SemaphoreType.DMA(())])
    return pl.pallas_call(
        _dispatch_kernel, out_shape=jax.ShapeDtypeStruct((n_rows, D), F32), grid_spec=grid_spec,
        input_output_aliases={4: 0},
        compiler_params=_cparams(("arbitrary",)), name="moe_dispatch",
    )(seg_start, eid, rank, hn, jnp.zeros((n_rows, D), F32))


def _expert_kernel(blk_e_ref, xs_ref, w1_ref, w3_ref, w2_ref, ys_ref):
    del blk_e_ref
    xb = xs_ref[...].astype(BF16)
    hid = jax.nn.silu(_dot(xb, w1_ref[0])) * _dot(xb, w3_ref[0])
    ys_ref[...] = _dot(hid.astype(BF16), w2_ref[0])


def _experts(blk_e, xs, w1, w3, w2):
    P, D = xs.shape
    n_blocks = P // MOE_ROWS
    grid_spec = pltpu.PrefetchScalarGridSpec(
        num_scalar_prefetch=1, grid=(n_blocks,),
        in_specs=[pl.BlockSpec((MOE_ROWS, D), lambda i, e: (i, 0)),
                  pl.BlockSpec((1, D, D_EXPERT), lambda i, e: (e[i], 0, 0)),
                  pl.BlockSpec((1, D, D_EXPERT), lambda i, e: (e[i], 0, 0)),
                  pl.BlockSpec((1, D_EXPERT, D), lambda i, e: (e[i], 0, 0))],
        out_specs=pl.BlockSpec((MOE_ROWS, D), lambda i, e: (i, 0)))
    return pl.pallas_call(
        _expert_kernel, out_shape=jax.ShapeDtypeStruct((P, D), F32), grid_spec=grid_spec,
        compiler_params=_cparams(("arbitrary",)), name="moe_experts")(blk_e, xs, w1, w3, w2)


def _combine_kernel(seg_ref, eid_ref, rank_ref, h_ref, wcol_ref, ys_ref, out_ref, g_ref, sem):
    tm = h_ref.shape[0]

    def row_copy(t, k):
        dest = seg_ref[eid_ref[k, t]] + rank_ref[k, t]
        return pltpu.make_async_copy(ys_ref.at[pl.ds(dest, 1), :], g_ref.at[k, pl.ds(t, 1), :], sem)

    def start(t, c):
        row_copy(t, 0).start()
        row_copy(t, 1).start()
        return c

    def wait(t, c):
        row_copy(t, 0).wait()
        row_copy(t, 1).wait()
        return c

    lax.fori_loop(0, tm, start, 0, unroll=8)
    lax.fori_loop(0, tm, wait, 0, unroll=8)
    w = wcol_ref[...]
    out_ref[...] = h_ref[...] + (g_ref[0] * w[:, 0:1] + g_ref[1] * w[:, 1:2])


def _combine(seg_start, eid, rank, h, wcol, ys, tm):
    N, D = h.shape
    smem_rows = pl.BlockSpec((2, tm), lambda i, seg: (0, i), memory_space=pltpu.SMEM)
    grid_spec = pltpu.PrefetchScalarGridSpec(
        num_scalar_prefetch=1, grid=(N // tm,),
        in_specs=[smem_rows, smem_rows, pl.BlockSpec((tm, D), lambda i, seg: (i, 0)),
                  pl.BlockSpec((tm, LANES), lambda i, seg: (i, 0)), pl.BlockSpec(memory_space=pl.ANY)],
        out_specs=pl.BlockSpec((tm, D), lambda i, seg: (i, 0)),
        scratch_shapes=[pltpu.VMEM((2, tm, D), F32), pltpu.SemaphoreType.DMA(())])
    return pl.pallas_call(
        _combine_kernel, out_shape=jax.ShapeDtypeStruct((N, D), F32), grid_spec=grid_spec,
        compiler_params=_cparams(("arbitrary",)), name="moe_combine")(seg_start, eid, rank, h, wcol, ys)


def _merge_and_ffn(x, ya, yb, ga, gb, mw, ew, tm):
    N, D = x.shape
    h, hn, eid, rank, wcol, counts = _merge(x, ya, yb, ga, gb, mw, tm)
    counts = counts[:, 0].astype(I32)
    padded = (counts + MOE_ROWS - 1) // MOE_ROWS * MOE_ROWS
    seg_end = jnp.cumsum(padded)
    seg_start = (seg_end - padded).astype(I32)
    n_blocks = -(-(2 * N) // MOE_ROWS) + N_EXPERTS
    block_row = jnp.arange(n_blocks, dtype=I32) * MOE_ROWS
    blk_e = jnp.minimum(jnp.sum(seg_end[None, :] <= block_row[:, None], axis=1), N_EXPERTS - 1).astype(I32)
    xs = _dispatch(seg_start, eid, rank, hn, n_blocks * MOE_ROWS, tm)
    ys = _experts(blk_e, xs, ew["w1"], ew["w3"], ew["w2"])
    return _combine(seg_start, eid, rank, h, wcol, ys, tm)


PROJ_TILE = 256


def kernel(x_prompt, x_sample, cache_a_k, cache_a_v, cache_idx_k, cache_mla_latent, cache_mla_krope, page_table,
           rel_bias, g_attn, w_in, b_gate, g_a_q, g_a_k, g_m_qa, w_uq, g_m_qn, g_m_qr, g_m_kva, g_m_kr, w_uk, g_m_kn,
           w_uv, w_oa, w_ob, w_out, g_ffn, w_group, b_group, w_expert, b_expert, w_e1, w_e3, w_e2):
    B, S, D = x_prompt.shape
    l = 0
    pw = _proj_weights(g_attn[l], w_in[l], b_gate[l], g_a_q[l], g_a_k[l], g_m_qa[l], w_uq[l], g_m_qn[l], g_m_qr[l],
                       g_m_kva[l], g_m_kr[l], w_uk[l], g_m_kn[l], w_uv[l])
    pp = _proj(x_prompt, jnp.arange(S, dtype=I32), pw, PROJ_TILE)
    ya_p = _dsa_prompt(pp, _bias_tiles(rel_bias), PROJ_TILE)
    yb_p = _mla_prompt(pp, PROJ_TILE)
    mw = _merge_weights(w_oa[l], w_ob[l], w_out[l], g_ffn[l], w_group[l], b_group[l], w_expert[l], b_expert[l])
    ew = dict(w1=w_e1[l].astype(BF16), w3=w_e3[l].astype(BF16), w2=w_e2[l].astype(BF16))
    flat = lambda a: a.reshape(B * S, a.shape[-1])
    y_p = _merge_and_ffn(flat(x_prompt), flat(ya_p), flat(yb_p), flat(pp["ga"]), flat(pp["gb"]), mw, ew, PROJ_TILE)

    DB = x_sample.shape[0]
    n_pages = page_table.shape[1]
    past = n_pages * PAGE_SIZE
    ps = _proj(x_sample.reshape(1, DB, D), jnp.full((DB,), past, I32), pw, DB)
    ps = {k: v[0] for k, v in ps.items()}
    pt_flat = page_table.reshape(-1).astype(I32)
    heads3 = lambda a: a.reshape(DB, 8, LANES)
    first_row = lambda a: jnp.pad(a[:, None, :], ((0, 0), (0, PAGE_SIZE - 1), (0, 0)))
    first_col = lambda a: jnp.pad(a[:, :, None], ((0, 0), (0, 0), (0, PAGE_SIZE - 1)))
    idx_t = jnp.swapaxes(cache_idx_k[l:l + 1], 2, 3)
    krope_t = jnp.swapaxes(cache_mla_krope[l:l + 1], 2, 3)
    ak_t = jnp.transpose(cache_a_k[l:l + 1], (0, 1, 3, 4, 2))
    av_t = jnp.transpose(cache_a_v[l:l + 1], (0, 1, 3, 4, 2))
    sc, sc_new = _idx_sample(pt_flat, heads3(ps["iqp"]), ps["iwT"], first_col(ps["ik"]), idx_t, n_pages)
    scores = jnp.concatenate([sc.reshape(DB, past), sc_new.reshape(DB, LANES)], axis=1)
    topk = min(TOPK_MAX, (past + 1) // 4)
    sel = _sel_sample(scores.T, topk).T.reshape(DB, n_pages + 1, LANES)
    slot = jnp.arange(PAGE_SIZE, dtype=I32)
    bias2 = jnp.stack([rel_bias[_rel_bucket(PAGE_SIZE - slot)].T,
                       rel_bias[_rel_bucket(2 * PAGE_SIZE - slot)].T]).astype(F32)
    ya_s = _dsa_sample(pt_flat, heads3(ps["aqp"]), sel[:, :n_pages], sel[:, n_pages:], ps["ak"].reshape(DB, 1, 256),
                       ps["av"].reshape(DB, 1, 256), bias2, rel_bias[0].reshape(A_HEADS, 1).astype(F32),
                       ak_t, av_t, n_pages).reshape(DB, A_HEADS * A_HEAD_DIM)
    qgt = (ps["qn"].astype(F32) * jnp.tile(g_m_kn[l], M_HEADS)[None, :]).T
    wukt = w_uk[l].reshape(M_KV_RANK, M_HEADS * M_NOPE).T.astype(BF16)
    yb_s = _mla_sample(pt_flat, qgt, heads3(ps["qr"]), first_row(ps["ckv"]), first_col(ps["kr"]), wukt, pw["wuv"],
                       cache_mla_latent[l:l + 1], krope_t, n_pages).reshape(DB, M_HEADS * M_V)
    y_s = _merge_and_ffn(x_sample.reshape(DB, D), ya_s.astype(BF16), yb_s.astype(BF16), ps["ga"], ps["gb"], mw, ew, DB)

    kv = lambda a, n: a.reshape(1, n, -1, A_KV_HEADS, A_HEAD_DIM)
    return (y_p.reshape(B, S, D), y_s.reshape(DB, 1, D),
            kv(pp["ak"], B), kv(pp["av"], B), pp["ik"][None], pp["ckv"][None], pp["kr"][None],
            kv(ps["ak"], DB), kv(ps["av"], DB), ps["ik"].reshape(1, DB, 1, IDX_DIM),
            ps["ckv"].reshape(1, DB, 1, M_KV_RANK), ps["kr"].reshape(1, DB, 1, M_ROPE))
```

```python
import functools
import math

import jax
import jax.numpy as jnp
import numpy as np
from jax import lax
from jax.experimental import pallas as pl
from jax.experimental.pallas import tpu as pltpu

F32 = jnp.float32
BF16 = jnp.bfloat16
I32 = jnp.int32

LANES = 128
EPS = 1e-6
A_HEADS = 8
A_KV_HEADS = 4
A_HEAD_DIM = 64
IDX_HEADS = 8
IDX_DIM = 64
TOPK_MAX = 256
INDEX_SCALE = (IDX_HEADS ** -0.5) * (IDX_DIM ** -0.5)
M_HEADS = 8
M_Q_RANK = 384
M_KV_RANK = 256
M_NOPE = 64
M_ROPE = 32
M_V = 64
MLA_SCALE = (M_NOPE + M_ROPE) ** -0.5
ROPE_BASE = 10000.0
REL_BUCKETS = 32
REL_MAX_DIST = 128
N_GROUPS = 4
EXPERTS_PER_GROUP = 8
N_EXPERTS = 32
D_EXPERT = 256
MOE_BLOCK = 128
Q_BLOCK = 128
PAGE_SIZE = 128
IN_SPLITS = (512, 256, 256, 512, 8, 64, 384, 256, 32, 1024, 1024)
NEG_INF = float("-inf")
VMEM_LIMIT = 56 * 1024 * 1024


def _dot(a, b):
    return jnp.dot(a, b, preferred_element_type=F32)


def _dot_nt(a, b):
    return lax.dot_general(a, b, (((1,), (1,)), ((), ())), preferred_element_type=F32)


def _cparams(sem):
    return pltpu.CompilerParams(dimension_semantics=sem, vmem_limit_bytes=VMEM_LIMIT)


def _expand_heads(z):
    lane = lax.broadcasted_iota(I32, (z.shape[0], LANES), 1)
    lo = lane < 64
    parts = []
    for j in range(z.shape[1] // LANES):
        c = z[:, j * LANES:(j + 1) * LANES]
        parts.append(jnp.where(lo, c, 0.0))
        parts.append(jnp.where(lo, pltpu.roll(c, 64, 1), 0.0))
    return jnp.concatenate(parts, axis=1)


def _rope_slot(r, c_tab, s1_tab, s2_tab):
    return r * c_tab + pltpu.roll(r, LANES - 16, 1) * s1_tab + pltpu.roll(r, 16, 1) * s2_tab


def _proj_kernel(x_ref, ctab_ref, s1tab_ref, s2tab_ref, gattn_ref, w_ref, bgate_ref, e64_ref,
                 gaq_ref, gak_ref, gqa_ref, wuq_ref, gqn_ref, gqr_ref, gkva_ref, gkr_ref,
                 wuk_ref, gkn_ref, wuv_ref,
                 ak_o, av_o, ik_o, ckv_o, kr_o,
                 aqp_o, akp_o, avt_o, iqp_o, iwt_o, ikb_o, qn_o, qr_o, kp_o, vmt_o, ga_o, gb_o):
    x = x_ref[0]
    ms = jnp.mean(x * x, axis=-1, keepdims=True)
    xn = x * lax.rsqrt(ms + EPS) * gattn_ref[...]
    xb = xn.astype(BF16)
    ctab, s1tab, s2tab = ctab_ref[...], s1tab_ref[...], s2tab_ref[...]

    def head_norm(z, width):
        ss = _dot((z * z).astype(BF16), e64_ref[:width, :width])
        return z * lax.rsqrt(ss * (1.0 / 64.0) + EPS)

    z = _dot(xb, w_ref[:, 0:512])
    zn = head_norm(z, 512) * gaq_ref[...] * (A_HEAD_DIM ** -0.5)
    aqp_o[0] = _expand_heads(zn).astype(BF16)
    z = _dot(xb, w_ref[:, 512:768])
    zn = head_norm(z, 256) * gak_ref[...]
    ak_o[0] = zn
    akp_o[0] = _expand_heads(zn).astype(BF16)
    z = _dot(xb, w_ref[:, 768:1024])
    av_o[0] = z
    avt_o[0, 0] = z.T.astype(BF16)
    z = _dot(xb, w_ref[:, 1024:1536])
    iqp_o[0] = _expand_heads(z).astype(BF16)
    z = _dot(xb, w_ref[:, 1536:1664]) * INDEX_SCALE
    iwt_o[0] = z.T[0:IDX_HEADS, :]
    z = _dot(xb, w_ref[:, 1664:1792])
    ik_o[0] = z[:, 0:IDX_DIM]
    ikb_o[0] = z.astype(BF16)
    z = _dot(xb, w_ref[:, 1792:2176])
    cq = z * lax.rsqrt(jnp.mean(z * z, axis=-1, keepdims=True) + EPS) * gqa_ref[...]
    cqb = cq.astype(BF16)
    z = _dot(cqb, wuq_ref[:, 0:512])
    qn_o[0] = (head_norm(z, 512) * gqn_ref[...] * MLA_SCALE).astype(BF16)
    parts = []
    for h in range(M_HEADS):
        r = _dot(cqb, wuq_ref[:, 512 + h * LANES:512 + (h + 1) * LANES])
        ss = jnp.sum(r * r, axis=-1, keepdims=True) * (1.0 / M_ROPE)
        rn = r * lax.rsqrt(ss + EPS) * gqr_ref[...]
        parts.append((_rope_slot(rn, ctab, s1tab, s2tab) * MLA_SCALE).astype(BF16))
    qr_o[0] = jnp.concatenate(parts, axis=1)
    z = _dot(xb, w_ref[:, 2176:2432])
    ckv = z * lax.rsqrt(jnp.mean(z * z, axis=-1, keepdims=True) + EPS) * gkva_ref[...]
    ckv_o[0] = ckv
    cb = ckv.astype(BF16)
    kn = _dot(cb, wuk_ref[...])
    knb = (head_norm(kn, 512) * gkn_ref[...]).astype(BF16)
    vmt_o[0, 0] = _dot(cb, wuv_ref[...]).T.astype(BF16)
    z = _dot(xb, w_ref[:, 2432:2560])
    ss = jnp.sum(z * z, axis=-1, keepdims=True) * (1.0 / M_ROPE)
    kr = _rope_slot(z * lax.rsqrt(ss + EPS) * gkr_ref[...], ctab, s1tab, s2tab)
    kr_o[0] = kr[:, 0:M_ROPE]
    krb = kr.astype(BF16)
    kp_o[0] = jnp.concatenate([p for j in range(4) for p in (knb[:, j * LANES:(j + 1) * LANES], krb)], axis=1)
    z = _dot(xb, w_ref[:, 2560:3584]) + bgate_ref[0:1, :]
    ga_o[0] = jax.nn.sigmoid(z).astype(BF16)
    z = _dot(xb, w_ref[:, 3584:4608]) + bgate_ref[1:2, :]
    gb_o[0] = jax.nn.sigmoid(z).astype(BF16)


def _proj_weights(g_attn, w_in, b_gate, g_a_q, g_a_k, g_m_qa, w_uq, g_m_qn, g_m_qr, g_m_kva, g_m_kr,
                  w_uk, g_m_kn, w_uv):
    D = w_in.shape[0]
    o = [int(v) for v in np.cumsum(IN_SPLITS)]

    def padc(a, width=LANES):
        return jnp.pad(a, ((0, 0), (0, width - a.shape[1])))

    w1 = jnp.concatenate([
        w_in[:, 0:o[3]], padc(w_in[:, o[3]:o[4]]), padc(w_in[:, o[4]:o[5]]), w_in[:, o[5]:o[7]],
        padc(w_in[:, o[7]:o[8]]), w_in[:, o[8]:o[10]]], axis=1).astype(BF16)
    wq = w_uq.reshape(M_Q_RANK, M_HEADS, M_NOPE + M_ROPE)
    wuq = jnp.concatenate(
        [wq[:, :, :M_NOPE].reshape(M_Q_RANK, M_HEADS * M_NOPE)]
        + [padc(wq[:, h, M_NOPE:]) for h in range(M_HEADS)], axis=1).astype(BF16)
    e64 = jnp.asarray(np.kron(np.eye(8, dtype=np.float32), np.ones((64, 64), np.float32)), BF16)
    row = lambda v: v.reshape(1, -1).astype(F32)
    return dict(
        gattn=row(g_attn), w1=w1, bgate=b_gate.astype(F32), e64=e64,
        gaq=row(jnp.tile(g_a_q, A_HEADS)), gak=row(jnp.tile(g_a_k, A_KV_HEADS)), gqa=row(g_m_qa), wuq=wuq,
        gqn=row(jnp.tile(g_m_qn, M_HEADS)), gqr=padc(row(g_m_qr)), gkva=row(g_m_kva), gkr=padc(row(g_m_kr)),
        wuk=w_uk.reshape(M_KV_RANK, M_HEADS * M_NOPE).astype(BF16), gkn=row(jnp.tile(g_m_kn, M_HEADS)),
        wuv=w_uv.reshape(M_KV_RANK, M_HEADS * M_V).astype(BF16))


def _rope_tables(pos):
    half = M_ROPE // 2
    inv = ROPE_BASE ** (-jnp.arange(half, dtype=F32) / half)
    ang = pos.astype(F32)[:, None] * inv
    cos, sin = jnp.cos(ang), jnp.sin(ang)
    n = pos.shape[0]
    ctab = jnp.concatenate([cos, cos, jnp.ones((n, LANES - 2 * half), F32)], axis=1)
    s1tab = jnp.concatenate([-sin, jnp.zeros((n, LANES - half), F32)], axis=1)
    s2tab = jnp.concatenate([jnp.zeros((n, half), F32), sin, jnp.zeros((n, LANES - 2 * half), F32)], axis=1)
    return ctab, s1tab, s2tab


def _proj(x, pos, pw, tm):
    B, S, D = x.shape
    ctab, s1tab, s2tab = _rope_tables(pos)
    tok = lambda w: pl.BlockSpec((1, tm, w), lambda s, b: (b, s, 0))
    tokT = lambda w: pl.BlockSpec((1, w, tm), lambda s, b: (b, 0, s))
    tokC = lambda w: pl.BlockSpec((1, 1, w, tm), lambda s, b: (b, s, 0, 0))
    tab = pl.BlockSpec((tm, LANES), lambda s, b: (s, 0))
    full = lambda a: pl.BlockSpec(a.shape, lambda s, b: (0,) * a.ndim)
    wnames = ["gattn", "w1", "bgate", "e64", "gaq", "gak", "gqa", "wuq", "gqn", "gqr", "gkva", "gkr", "wuk", "gkn",
              "wuv"]
    ws = [pw[n] for n in wnames]
    sds = lambda shape, dt: jax.ShapeDtypeStruct(shape, dt)
    out_shape = [
        sds((B, S, 256), F32), sds((B, S, 256), F32), sds((B, S, IDX_DIM), F32), sds((B, S, M_KV_RANK), F32),
        sds((B, S, M_ROPE), F32),
        sds((B, S, 1024), BF16), sds((B, S, 512), BF16), sds((B, S // tm, 256, tm), BF16), sds((B, S, 1024), BF16),
        sds((B, IDX_HEADS, S), F32), sds((B, S, LANES), BF16), sds((B, S, 512), BF16), sds((B, S, 1024), BF16),
        sds((B, S, 1024), BF16), sds((B, S // tm, 512, tm), BF16), sds((B, S, D), BF16), sds((B, S, D), BF16)]
    out_specs = [tok(256), tok(256), tok(IDX_DIM), tok(M_KV_RANK), tok(M_ROPE),
                 tok(1024), tok(512), tokC(256), tok(1024), tokT(IDX_HEADS), tok(LANES), tok(512), tok(1024),
                 tok(1024), tokC(512), tok(D), tok(D)]
    outs = pl.pallas_call(
        _proj_kernel, out_shape=out_shape, grid=(S // tm, B),
        in_specs=[tok(D), tab, tab, tab] + [full(a) for a in ws], out_specs=out_specs,
        compiler_params=_cparams(("arbitrary", "arbitrary")), name="proj")(x, ctab, s1tab, s2tab, *ws)
    names = ["ak", "av", "ik", "ckv", "kr", "aqp", "akp", "avT", "iqp", "iwT", "ikb", "qn", "qr", "kp", "vmT",
             "ga", "gb"]
    return dict(zip(names, outs))


INT_MIN = -2 ** 31


def _rel_bucket(dist):
    n = jnp.maximum(dist, 0)
    max_exact = REL_BUCKETS // 2
    nf = jnp.maximum(n, 1).astype(F32)
    large = max_exact + (jnp.log(nf / max_exact) / math.log(REL_MAX_DIST / max_exact)
                         * (REL_BUCKETS - max_exact)).astype(I32)
    large = jnp.minimum(large, REL_BUCKETS - 1)
    return jnp.where(n < max_exact, n, large)


def _bias_tiles(rel_bias):
    qq = jnp.arange(Q_BLOCK, dtype=I32)[None, :]
    kk = jnp.arange(Q_BLOCK, dtype=I32)[:, None]
    tiles = [jnp.moveaxis(rel_bias[_rel_bucket(o * Q_BLOCK + qq - kk)], -1, 0) for o in range(3)]
    return jnp.stack(tiles).astype(F32)


def _sortable_key(sc):
    bits = pltpu.bitcast(sc, I32)
    key = jnp.where(bits < 0, bits ^ jnp.int32(0x7FFFFFFF), bits)
    return jnp.where(bits == jnp.int32(INT_MIN), 0, key)


def _colsum8(m):
    return m.reshape(m.shape[0] // 8, 8, LANES).sum(axis=0)


def _select_topk_columns(key_ref, n_chunks, kb, topk, active, idx_bits):
    def count(pred):
        def body(c, acc):
            r0 = pl.multiple_of(c * kb, kb)
            k = key_ref[pl.ds(r0, kb), :]
            return acc + _colsum8(pred(k, r0).astype(I32))
        acc = lax.fori_loop(0, n_chunks, body, jnp.zeros((8, LANES), I32))
        return jnp.sum(acc, axis=0, keepdims=True)

    zero = jnp.zeros((1, LANES), I32)
    cnt0 = count(lambda k, r0: k >= zero)
    t0 = jnp.where(jnp.logical_and(cnt0 >= topk, active), 0, INT_MIN).astype(I32)

    def bit_body(j, t):
        cand = t | jnp.left_shift(jnp.int32(1), 30 - j)
        cnt = count(lambda k, r0: k >= cand)
        return jnp.where(cnt >= topk, cand, t)

    t = lax.fori_loop(0, jnp.where(active, 31, 0), bit_body, t0)
    need = topk - count(lambda k, r0: k > t)
    excess = jnp.max(count(lambda k, r0: k == t) - need) > 0
    tie_search = jnp.logical_and(active, excess)

    def tie_body(j, cut):
        cand = cut | jnp.left_shift(jnp.int32(1), idx_bits - 1 - j)

        def pred(k, r0):
            row = r0 + lax.broadcasted_iota(I32, (kb, LANES), 0)
            return jnp.logical_and(k == t, row < cand)
        return jnp.where(count(pred) <= need, cand, cut)

    cut0 = jnp.where(tie_search, zero, jnp.int32(2 ** idx_bits - 1))
    cut = lax.fori_loop(0, jnp.where(tie_search, idx_bits, 0), tie_body, cut0)
    return t, cut


def _softmax_step(s, m_old, l_old):
    m_new = jnp.maximum(m_old, jnp.max(s, axis=0, keepdims=True))
    m_safe = jnp.where(m_new == NEG_INF, 0.0, m_new)
    p = jnp.exp(s - m_safe)
    alpha = jnp.exp(m_old - m_safe)
    l_new = alpha * l_old + jnp.sum(p, axis=0, keepdims=True)
    return p, alpha, m_new, l_new


def _dsa_prompt_kernel(aq_ref, iq_ref, iw_ref, ak_ref, ik_ref, avt_ref, bias_ref, y_ref,
                       sc_ref, key_ref, acc_ref, m_ref, l_ref, *, topk, kb, idx_bits):
    i = pl.program_id(1)
    n_keys = (i + 1) * Q_BLOCK
    n_chunks = (n_keys + kb - 1) // kb
    q_pos = i * Q_BLOCK + lax.broadcasted_iota(I32, (kb, LANES), 1)
    row_iota = lax.broadcasted_iota(I32, (kb, LANES), 0)

    iq = iq_ref[0]
    half = IDX_HEADS // 2
    iq_stacks = [jnp.concatenate([iq[:, h * LANES:(h + 1) * LANES] for h in range(u * half, (u + 1) * half)], axis=0)
                 for u in range(2)]
    iw = iw_ref[0]

    def score_body(c, carry):
        r0 = pl.multiple_of(c * kb, kb)
        ik = ik_ref[0, pl.ds(r0, kb), :]
        d = [_dot_nt(ik, iq_stacks[u]) for u in range(2)]
        sc = None
        for h in range(IDX_HEADS):
            term = iw[h:h + 1, :] * jnp.maximum(d[h // half][:, (h % half) * LANES:(h % half + 1) * LANES], 0.0)
            sc = term if sc is None else sc + term
        sc = jnp.where(r0 + row_iota <= q_pos, sc, NEG_INF)
        key_ref[pl.ds(r0, kb), :] = _sortable_key(sc)
        return carry

    lax.fori_loop(0, n_chunks, score_body, 0)
    t, cut = _select_topk_columns(key_ref, n_chunks, kb, topk, n_keys > topk, idx_bits)

    def mask_body(c, carry):
        r0 = pl.multiple_of(c * kb, kb)
        k = key_ref[pl.ds(r0, kb), :]
        row = r0 + row_iota
        sel = jnp.logical_or(k > t, jnp.logical_and(k == t, row < cut))
        sc_ref[pl.ds(r0, kb), :] = jnp.where(jnp.logical_and(sel, row <= q_pos), 0.0, NEG_INF)
        return carry

    lax.fori_loop(0, n_chunks, mask_body, 0)

    aq = aq_ref[0]
    aq_stack = [jnp.concatenate([aq[:, (2 * g) * LANES:(2 * g + 1) * LANES],
                                 aq[:, (2 * g + 1) * LANES:(2 * g + 2) * LANES]], axis=0)
                for g in range(A_KV_HEADS)]
    m_ref[...] = jnp.full(m_ref.shape, NEG_INF, F32)
    l_ref[...] = jnp.zeros(l_ref.shape, F32)
    acc_ref[...] = jnp.zeros(acc_ref.shape, F32)
    n_sub = kb // Q_BLOCK

    def scores(c):
        r0 = pl.multiple_of(c * kb, kb)
        return [_dot_nt(ak_ref[0, pl.ds(r0, kb), g * LANES:(g + 1) * LANES], aq_stack[g])
                for g in range(A_KV_HEADS)]

    def attn_body(c, s_cur):
        s_next = scores(jnp.minimum(c + 1, n_chunks - 1))
        r0 = pl.multiple_of(c * kb, kb)
        addm = sc_ref[pl.ds(r0, kb), :]
        tile_idx = [jnp.clip(i - (c * n_sub + u), 0, 2) for u in range(n_sub)]

        def softmax_and_pv(g, s_pair):
            ps, alphas = [], []
            for hh in range(2):
                h = 2 * g + hh
                bias = jnp.concatenate([bias_ref[tile_idx[u], h] for u in range(n_sub)], axis=0)
                s = s_pair[:, hh * LANES:(hh + 1) * LANES] + bias + addm
                p, alpha, m_new, l_new = _softmax_step(s, m_ref[h:h + 1, :], l_ref[h:h + 1, :])
                m_ref[h:h + 1, :] = m_new
                l_ref[h:h + 1, :] = l_new
                ps.append(p.astype(BF16))
                alphas.append(alpha)
            o_pair = _dot(avt_ref[0, c, g * A_HEAD_DIM:(g + 1) * A_HEAD_DIM, :], jnp.concatenate(ps, axis=1))
            for hh in range(2):
                h = 2 * g + hh
                rows = slice(h * A_HEAD_DIM, (h + 1) * A_HEAD_DIM)
                acc_ref[rows, :] = alphas[hh] * acc_ref[rows, :] + o_pair[:, hh * LANES:(hh + 1) * LANES]

        for g in range(A_KV_HEADS):
            softmax_and_pv(g, s_cur[g])
        return s_next

    lax.fori_loop(0, n_chunks, attn_body, scores(0))
    outs = [acc_ref[h * A_HEAD_DIM:(h + 1) * A_HEAD_DIM, :] / l_ref[h:h + 1, :] for h in range(A_HEADS)]
    y_ref[0] = jnp.concatenate(outs, axis=0).T.astype(BF16)


def _dsa_prompt(p, bias_tiles, kb):
    B, S, _ = p["aqp"].shape
    topk = min(TOPK_MAX, S // 4)
    nqb = S // Q_BLOCK
    idx_bits = int(S).bit_length()
    qblk = lambda w: pl.BlockSpec((1, Q_BLOCK, w), lambda b, i: (b, i, 0))
    kern = functools.partial(_dsa_prompt_kernel, topk=topk, kb=kb, idx_bits=idx_bits)
    return pl.pallas_call(
        kern, out_shape=jax.ShapeDtypeStruct((B, S, A_HEADS * A_HEAD_DIM), BF16), grid=(B, nqb),
        in_specs=[qblk(1024), qblk(1024), pl.BlockSpec((1, IDX_HEADS, Q_BLOCK), lambda b, i: (b, 0, i)),
                  pl.BlockSpec((1, S, 512), lambda b, i: (b, 0, 0)),
                  pl.BlockSpec((1, S, LANES), lambda b, i: (b, 0, 0)),
                  pl.BlockSpec((1, S // kb, 256, kb), lambda b, i: (b, 0, 0, 0)),
                  pl.BlockSpec(bias_tiles.shape, lambda b, i: (0, 0, 0, 0))],
        out_specs=qblk(A_HEADS * A_HEAD_DIM),
        scratch_shapes=[pltpu.VMEM((S, LANES), F32), pltpu.VMEM((S, LANES), I32),
                        pltpu.VMEM((A_HEADS * A_HEAD_DIM, LANES), F32), pltpu.VMEM((A_HEADS, LANES), F32),
                        pltpu.VMEM((A_HEADS, LANES), F32)],
        compiler_params=_cparams(("arbitrary", "arbitrary")), name="dsa_prompt",
    )(p["aqp"], p["iqp"], p["iwT"], p["akp"], p["ikb"], p["avT"], bias_tiles)


def _causal_tiles():
    qq = np.arange(Q_BLOCK)[None, :]
    kk = np.arange(Q_BLOCK)[:, None]
    tri = np.where(kk <= qq, 0.0, -np.inf).astype(np.float32)
    return jnp.asarray(np.stack([np.zeros_like(tri), tri, np.full_like(tri, -np.inf)]))


def _mla_prompt_kernel(qn_ref, qr_ref, kp_ref, vmt_ref, mask_ref, y_ref, acc_ref, m_ref, l_ref, *, kb):
    i = pl.program_id(1)
    n_chunks = ((i + 1) * Q_BLOCK + kb - 1) // kb
    n_sub = kb // Q_BLOCK
    lo = lax.broadcasted_iota(I32, (Q_BLOCK, LANES), 1) < M_NOPE
    qn, qr = qn_ref[0], qr_ref[0]
    zero = jnp.zeros((Q_BLOCK, LANES), BF16)
    q_stack = []
    for j in range(M_HEADS // 2):
        c = qn[:, j * LANES:(j + 1) * LANES]
        top = jnp.concatenate([jnp.where(lo, c, zero), qr[:, (2 * j) * LANES:(2 * j + 1) * LANES]], axis=1)
        bot = jnp.concatenate([jnp.where(lo, zero, c), qr[:, (2 * j + 1) * LANES:(2 * j + 2) * LANES]], axis=1)
        q_stack.append(jnp.concatenate([top, bot], axis=0))
    m_ref[...] = jnp.full(m_ref.shape, NEG_INF, F32)
    l_ref[...] = jnp.zeros(l_ref.shape, F32)
    acc_ref[...] = jnp.zeros(acc_ref.shape, F32)

    def scores(c):
        r0 = pl.multiple_of(c * kb, kb)
        return [_dot_nt(kp_ref[0, pl.ds(r0, kb), j * 2 * LANES:(j + 1) * 2 * LANES], q_stack[j])
                for j in range(M_HEADS // 2)]

    def attn_body(c, s_cur):
        s_next = scores(jnp.minimum(c + 1, n_chunks - 1))
        addm = jnp.concatenate([mask_ref[jnp.clip(c * n_sub + u - i + 1, 0, 2)] for u in range(n_sub)], axis=0)
        for h in range(M_HEADS):
            s = s_cur[h // 2][:, (h % 2) * LANES:(h % 2 + 1) * LANES] + addm
            p, alpha, m_new, l_new = _softmax_step(s, m_ref[h:h + 1, :], l_ref[h:h + 1, :])
            m_ref[h:h + 1, :] = m_new
            l_ref[h:h + 1, :] = l_new
            rows = slice(h * M_V, (h + 1) * M_V)
            o = _dot(vmt_ref[0, c, rows, :], p.astype(BF16))
            acc_ref[rows, :] = alpha * acc_ref[rows, :] + o
        return s_next

    lax.fori_loop(0, n_chunks, attn_body, scores(0))
    outs = [acc_ref[h * M_V:(h + 1) * M_V, :] / l_ref[h:h + 1, :] for h in range(M_HEADS)]
    y_ref[0] = jnp.concatenate(outs, axis=0).T.astype(BF16)


def _mla_prompt(p, kb):
    B, S, _ = p["qn"].shape
    qblk = lambda w: pl.BlockSpec((1, Q_BLOCK, w), lambda b, i: (b, i, 0))
    masks = _causal_tiles()
    return pl.pallas_call(
        functools.partial(_mla_prompt_kernel, kb=kb),
        out_shape=jax.ShapeDtypeStruct((B, S, M_HEADS * M_V), BF16), grid=(B, S // Q_BLOCK),
        in_specs=[qblk(512), qblk(1024), pl.BlockSpec((1, S, 1024), lambda b, i: (b, 0, 0)),
                  pl.BlockSpec((1, S // kb, 512, kb), lambda b, i: (b, 0, 0, 0)),
                  pl.BlockSpec(masks.shape, lambda b, i: (0, 0, 0))],
        out_specs=qblk(M_HEADS * M_V),
        scratch_shapes=[pltpu.VMEM((M_HEADS * M_V, LANES), F32), pltpu.VMEM((M_HEADS, LANES), F32),
                        pltpu.VMEM((M_HEADS, LANES), F32)],
        compiler_params=_cparams(("arbitrary", "arbitrary")), name="mla_prompt",
    )(p["qn"], p["qr"], p["kp"], p["vmT"], masks)


PAGES_PER_STEP = 32


def _page_specs(shape_tail, n_pages, per_step=PAGES_PER_STEP):
    def spec(u):
        def index_map(b, j, pt):
            return (0, pt[b * n_pages + j * per_step + u]) + (0,) * len(shape_tail)
        return pl.BlockSpec((1, 1) + tuple(shape_tail), index_map)
    return [spec(u) for u in range(per_step)]


def _lane_column(mat, b):
    lane = lax.broadcasted_iota(I32, mat.shape, 1)
    return jnp.sum(jnp.where(lane == b, mat, 0.0), axis=1, keepdims=True)


IDX_PAGES_PER_STEP = 32


def _idx_sample_kernel(pt_ref, iq_ref, iwt_ref, iknew_ref, *rest):
    del pt_ref
    pages = rest[:IDX_PAGES_PER_STEP]
    sc_o, scnew_o = rest[IDX_PAGES_PER_STEP:]
    b, j = pl.program_id(0), pl.program_id(1)
    iq = iq_ref[0][:, 0:IDX_DIM]
    wcol = _lane_column(iwt_ref[...], b)

    def scores(keys_t):
        d = _dot(iq, keys_t.astype(BF16))
        return jnp.sum(wcol * jnp.maximum(d, 0.0), axis=0, keepdims=True)

    sc_o[0] = scores(jnp.concatenate([pages[u][0, 0] for u in range(IDX_PAGES_PER_STEP)], axis=1))

    @pl.when(j == 0)
    def _():
        lane = lax.broadcasted_iota(I32, (1, LANES), 1)
        scnew_o[0] = jnp.where(lane == 0, scores(iknew_ref[0]), NEG_INF)


def _idx_sample(pt_flat, iq3, iwT, iknew_page_t, cache_idx_t, n_pages):
    DB = iq3.shape[0]
    per_step = IDX_PAGES_PER_STEP
    grid_spec = pltpu.PrefetchScalarGridSpec(
        num_scalar_prefetch=1, grid=(DB, n_pages // per_step),
        in_specs=[pl.BlockSpec((1, IDX_HEADS, LANES), lambda b, j, pt: (b, 0, 0)),
                  pl.BlockSpec(iwT.shape, lambda b, j, pt: (0, 0)),
                  pl.BlockSpec((1, IDX_DIM, PAGE_SIZE), lambda b, j, pt: (b, 0, 0))]
        + _page_specs((IDX_DIM, PAGE_SIZE), n_pages, per_step),
        out_specs=[pl.BlockSpec((1, 1, per_step * PAGE_SIZE), lambda b, j, pt: (b, 0, j)),
                   pl.BlockSpec((1, 1, LANES), lambda b, j, pt: (b, 0, 0))])
    return pl.pallas_call(
        _idx_sample_kernel,
        out_shape=[jax.ShapeDtypeStruct((DB, 1, n_pages * PAGE_SIZE), F32), jax.ShapeDtypeStruct((DB, 1, LANES), F32)],
        grid_spec=grid_spec, compiler_params=_cparams(("arbitrary", "arbitrary")), name="idx_sample",
    )(pt_flat, iq3, iwT, iknew_page_t, *([cache_idx_t] * per_step))


def _sel_sample_kernel(sc_ref, sel_ref, key_ref, *, topk, kb, idx_bits):
    n_rows = sc_ref.shape[0]
    n_chunks = n_rows // kb
    row_iota = lax.broadcasted_iota(I32, (kb, LANES), 0)

    def key_body(c, carry):
        r0 = pl.multiple_of(c * kb, kb)
        key_ref[pl.ds(r0, kb), :] = _sortable_key(sc_ref[pl.ds(r0, kb), :])
        return carry

    lax.fori_loop(0, n_chunks, key_body, 0)
    t, cut = _select_topk_columns(key_ref, n_chunks, kb, topk, jnp.bool_(True), idx_bits)

    def mask_body(c, carry):
        r0 = pl.multiple_of(c * kb, kb)
        k = key_ref[pl.ds(r0, kb), :]
        sel = jnp.logical_or(k > t, jnp.logical_and(k == t, r0 + row_iota < cut))
        sel_ref[pl.ds(r0, kb), :] = sel.astype(F32)
        return carry

    lax.fori_loop(0, n_chunks, mask_body, 0)


def _sel_sample(scores_t, topk):
    L, DB = scores_t.shape
    kern = functools.partial(_sel_sample_kernel, topk=topk, kb=PAGE_SIZE, idx_bits=int(L).bit_length())
    return pl.pallas_call(
        kern, out_shape=jax.ShapeDtypeStruct((L, DB), F32), grid=(1,),
        in_specs=[pl.BlockSpec((L, DB), lambda i: (0, 0))], out_specs=pl.BlockSpec((L, DB), lambda i: (0, 0)),
        scratch_shapes=[pltpu.VMEM((L, DB), I32)],
        compiler_params=_cparams(("arbitrary",)), name="sel_sample")(scores_t)


def _dsa_sample_kernel(pt_ref, aq_ref, sel_ref, selnew_ref, knew_ref, vnew_ref, bias_ref, bnew_ref, *rest, n_steps):
    del pt_ref
    pps = PAGES_PER_STEP
    k_pages, v_pages = rest[:pps], rest[pps:2 * pps]
    y_o, acc_ref, m_ref, l_ref = rest[2 * pps:]
    j = pl.program_id(1)
    kv_of_row = lax.broadcasted_iota(I32, (A_HEADS, LANES), 0) // (A_HEADS // A_KV_HEADS)
    aq = aq_ref[0]
    q64 = aq[:, 0:A_HEAD_DIM]

    def per_kv_head(pieces):
        out = pieces[A_KV_HEADS - 1]
        for g in range(A_KV_HEADS - 2, -1, -1):
            out = jnp.where(kv_of_row[:, 0:1] == g, pieces[g], out)
        return out

    @pl.when(j == 0)
    def _():
        kn = knew_ref[0].astype(BF16).astype(F32)
        vn = vnew_ref[0].astype(BF16).astype(F32)
        lane = lax.broadcasted_iota(I32, (1, LANES), 1)
        picked = jnp.sum(jnp.where(lane == 0, selnew_ref[0], 0.0), axis=1, keepdims=True) > 0.0
        qf = aq.astype(F32)
        halves = [kn[:, 0:LANES], pltpu.roll(kn[:, 0:LANES], 64, 1), kn[:, LANES:], pltpu.roll(kn[:, LANES:], 64, 1)]
        lo = lax.broadcasted_iota(I32, (A_HEADS, LANES), 1) < A_HEAD_DIM
        s_new = jnp.sum(jnp.where(lo, qf * per_kv_head(halves), 0.0), axis=1, keepdims=True) + bnew_ref[...]
        vh = [vn[:, 0:LANES], pltpu.roll(vn[:, 0:LANES], 64, 1), vn[:, LANES:], pltpu.roll(vn[:, LANES:], 64, 1)]
        m_ref[...] = jnp.where(picked, s_new, NEG_INF)
        l_ref[...] = jnp.where(picked, 1.0, 0.0) + jnp.zeros(l_ref.shape, F32)
        acc_ref[...] = jnp.where(picked, per_kv_head(vh)[:, 0:A_HEAD_DIM] + jnp.zeros(acc_ref.shape, F32), 0.0)

    kcat = [jnp.concatenate([k_pages[u][0, 0, g] for u in range(pps)], axis=1).astype(BF16) for g in range(A_KV_HEADS)]
    s = per_kv_head([_dot(q64, kcat[g]) for g in range(A_KV_HEADS)])
    last = j == n_steps - 1
    bias = jnp.concatenate([bias_ref[1]] * (pps - 1) + [bias_ref[jnp.where(last, 0, 1)]], axis=1)
    keep = jnp.concatenate([sel_ref[0, u:u + 1, :] for u in range(pps)], axis=1) > 0.0
    s = jnp.where(keep, s + bias, NEG_INF)
    m_old = m_ref[...]
    m_new = jnp.maximum(m_old, jnp.max(s, axis=1, keepdims=True))
    m_safe = jnp.where(m_new == NEG_INF, 0.0, m_new)
    p = jnp.exp(s - m_safe)
    alpha = jnp.exp(m_old - m_safe)
    l_ref[...] = alpha * l_ref[...] + jnp.sum(p, axis=1, keepdims=True)
    m_ref[...] = m_new
    pb = p.astype(BF16)
    vcat = [jnp.concatenate([v_pages[u][0, 0, g] for u in range(pps)], axis=1).astype(BF16) for g in range(A_KV_HEADS)]
    acc_ref[...] = alpha * acc_ref[...] + per_kv_head([_dot_nt(pb, vcat[g]) for g in range(A_KV_HEADS)])

    @pl.when(last)
    def _():
        y_o[0] = acc_ref[...] / l_ref[...]


def _dsa_sample(pt_flat, aq3, sel_past, sel_new, knew, vnew, bias2, bias_new, ck_t, cv_t, n_pages):
    DB = aq3.shape[0]
    n_steps = n_pages // PAGES_PER_STEP
    page_tail = (A_KV_HEADS, A_HEAD_DIM, PAGE_SIZE)
    grid_spec = pltpu.PrefetchScalarGridSpec(
        num_scalar_prefetch=1, grid=(DB, n_steps),
        in_specs=[pl.BlockSpec((1, A_HEADS, LANES), lambda b, j, pt: (b, 0, 0)),
                  pl.BlockSpec((1, PAGES_PER_STEP, LANES), lambda b, j, pt: (b, j, 0)),
                  pl.BlockSpec((1, 1, LANES), lambda b, j, pt: (b, 0, 0)),
                  pl.BlockSpec((1, 1, 256), lambda b, j, pt: (b, 0, 0)),
                  pl.BlockSpec((1, 1, 256), lambda b, j, pt: (b, 0, 0)),
                  pl.BlockSpec(bias2.shape, lambda b, j, pt: (0, 0, 0)),
                  pl.BlockSpec(bias_new.shape, lambda b, j, pt: (0, 0))]
        + _page_specs(page_tail, n_pages) + _page_specs(page_tail, n_pages),
        out_specs=pl.BlockSpec((1, A_HEADS, A_HEAD_DIM), lambda b, j, pt: (b, 0, 0)),
        scratch_shapes=[pltpu.VMEM((A_HEADS, A_HEAD_DIM), F32), pltpu.VMEM((A_HEADS, 1), F32),
                        pltpu.VMEM((A_HEADS, 1), F32)])
    return pl.pallas_call(
        functools.partial(_dsa_sample_kernel, n_steps=n_steps),
        out_shape=jax.ShapeDtypeStruct((DB, A_HEADS, A_HEAD_DIM), F32), grid_spec=grid_spec,
        compiler_params=_cparams(("arbitrary", "arbitrary")), name="dsa_sample",
    )(pt_flat, aq3, sel_past, sel_new, knew, vnew, bias2, bias_new,
      *([ck_t] * PAGES_PER_STEP), *([cv_t] * PAGES_PER_STEP))


def _mla_sample_kernel(pt_ref, qgt_ref, qr_ref, latnew_ref, krnew_ref, wukt_ref, wuv_ref, *rest,
                       n_steps):
    del pt_ref
    lat_pages = rest[:PAGES_PER_STEP]
    kr_pages = rest[PAGES_PER_STEP:2 * PAGES_PER_STEP]
    y_o, qcol_ref, acc_ref, m_ref, l_ref = rest[2 * PAGES_PER_STEP:]
    b, j = pl.program_id(0), pl.program_id(1)
    qr = qr_ref[0][:, 0:M_ROPE]

    def attend(lat, kr_t, valid):
        nk = lat.shape[0]
        latb = lat.astype(BF16)
        qcol = qcol_ref[...]
        step = min(nk, 2 * LANES)

        def expand(c):
            return _dot_nt(wukt_ref[...], latb[c * step:(c + 1) * step, :])

        def nope_scores(knt):
            rows = []
            for h in range(M_HEADS):
                kh = knt[h * M_NOPE:(h + 1) * M_NOPE, :]
                qh = qcol[h * M_NOPE:(h + 1) * M_NOPE, :]
                if step > LANES:
                    qh = jnp.concatenate([qh] * (step // LANES), axis=1)
                ss = jnp.sum(kh * kh, axis=0, keepdims=True) * (1.0 / M_NOPE)
                raw = jnp.sum(kh * qh, axis=0, keepdims=True)
                rows.append(raw * lax.rsqrt(ss + EPS))
            return jnp.concatenate(rows, axis=0)

        parts = []
        knt_next = expand(0)
        for c in range(nk // step):
            knt_cur = knt_next
            if c + 1 < nk // step:
                knt_next = expand(c + 1)
            parts.append(nope_scores(knt_cur))
        s = jnp.concatenate(parts, axis=1) + _dot(qr, kr_t.astype(BF16))
        if valid is not None:
            s = jnp.where(valid, s, NEG_INF)
        m_old = m_ref[...]
        m_new = jnp.maximum(m_old, jnp.max(s, axis=1, keepdims=True))
        m_safe = jnp.where(m_new == NEG_INF, 0.0, m_new)
        p = jnp.exp(s - m_safe)
        alpha = jnp.exp(m_old - m_safe)
        l_ref[...] = alpha * l_ref[...] + jnp.sum(p, axis=1, keepdims=True)
        m_ref[...] = m_new
        acc_ref[...] = alpha * acc_ref[...] + _dot(p.astype(BF16), latb)

    @pl.when(j == 0)
    def _():
        qcol_ref[...] = jnp.zeros(qcol_ref.shape, F32) + _lane_column(qgt_ref[...], b)
        m_ref[...] = jnp.full(m_ref.shape, NEG_INF, F32)
        l_ref[...] = jnp.zeros(l_ref.shape, F32)
        acc_ref[...] = jnp.zeros(acc_ref.shape, F32)
        lane = lax.broadcasted_iota(I32, (M_HEADS, LANES), 1)
        attend(latnew_ref[0], krnew_ref[0], lane == 0)

    attend(jnp.concatenate([lat_pages[u][0, 0] for u in range(PAGES_PER_STEP)], axis=0),
           jnp.concatenate([kr_pages[u][0, 0] for u in range(PAGES_PER_STEP)], axis=1), None)

    @pl.when(j == n_steps - 1)
    def _():
        lat_out = (acc_ref[...] / l_ref[...]).astype(BF16)
        full = _dot(lat_out, wuv_ref[...])
        head = lax.broadcasted_iota(I32, full.shape, 1) // M_V
        row = lax.broadcasted_iota(I32, full.shape, 0)
        y_o[0] = jnp.sum(jnp.where(head == row, full, 0.0), axis=0, keepdims=True)


def _mla_sample(pt_flat, qgt, qr3, latnew, krnew, wukt, wuv, cache_lat, cache_kr, n_pages):
    DB = qr3.shape[0]
    n_steps = n_pages // PAGES_PER_STEP
    grid_spec = pltpu.PrefetchScalarGridSpec(
        num_scalar_prefetch=1, grid=(DB, n_steps),
        in_specs=[pl.BlockSpec(qgt.shape, lambda b, j, pt: (0, 0)),
                  pl.BlockSpec((1, M_HEADS, LANES), lambda b, j, pt: (b, 0, 0)),
                  pl.BlockSpec((1, PAGE_SIZE, M_KV_RANK), lambda b, j, pt: (b, 0, 0)),
                  pl.BlockSpec((1, M_ROPE, PAGE_SIZE), lambda b, j, pt: (b, 0, 0)),
                  pl.BlockSpec(wukt.shape, lambda b, j, pt: (0, 0)),
                  pl.BlockSpec(wuv.shape, lambda b, j, pt: (0, 0))]
        + _page_specs((PAGE_SIZE, M_KV_RANK), n_pages) + _page_specs((M_ROPE, PAGE_SIZE), n_pages),
        out_specs=pl.BlockSpec((1, 1, M_HEADS * M_V), lambda b, j, pt: (b, 0, 0)),
        scratch_shapes=[pltpu.VMEM((M_HEADS * M_NOPE, LANES), F32), pltpu.VMEM((M_HEADS, M_KV_RANK), F32),
                        pltpu.VMEM((M_HEADS, 1), F32), pltpu.VMEM((M_HEADS, 1), F32)])
    return pl.pallas_call(
        functools.partial(_mla_sample_kernel, n_steps=n_steps),
        out_shape=jax.ShapeDtypeStruct((DB, 1, M_HEADS * M_V), F32), grid_spec=grid_spec,
        compiler_params=_cparams(("arbitrary", "arbitrary")), name="mla_sample",
    )(pt_flat, qgt, qr3, latnew, krnew, wukt, wuv, *([cache_lat] * PAGES_PER_STEP), *([cache_kr] * PAGES_PER_STEP))


ROUTER_ROWS = 40


def _merge_kernel(x_ref, ya_ref, yb_ref, ga_ref, gb_ref, woa_ref, wob_ref, wout_ref, gffn_ref,
                  wrh_ref, wrl_ref, br_ref, tri_ref,
                  h_o, hn_o, eid_o, rank_o, wcol_o, cnt_o, cnt_ref):
    tm = x_ref.shape[0]

    @pl.when(pl.program_id(0) == 0)
    def _():
        cnt_ref[...] = jnp.zeros(cnt_ref.shape, F32)

    merged = (ga_ref[...].astype(F32) * _dot(ya_ref[...], woa_ref[...])
              + gb_ref[...].astype(F32) * _dot(yb_ref[...], wob_ref[...]))
    h = x_ref[...] + _dot(merged.astype(BF16), wout_ref[...])
    h_o[...] = h
    hn = h * lax.rsqrt(jnp.mean(h * h, axis=-1, keepdims=True) + EPS) * gffn_ref[...]
    hn_o[...] = hn
    hh = hn.astype(BF16)
    hl = (hn - hh.astype(F32)).astype(BF16)
    logit = (_dot_nt(wrh_ref[...], hh) + _dot_nt(wrh_ref[...], hl) + _dot_nt(wrl_ref[...], hh)) + br_ref[...]
    row8 = lax.broadcasted_iota(I32, (8, tm), 0)
    gl = logit[0:8, :]
    gmax = jnp.max(gl, axis=0, keepdims=True)
    gtop = jnp.min(jnp.where(gl == gmax, row8, 8), axis=0, keepdims=True)
    pg = 1.0 / jnp.sum(jnp.exp(gl - gmax), axis=0, keepdims=True)
    elg = logit[8:16, :]
    for g in range(1, N_GROUPS):
        elg = jnp.where(gtop == g, logit[8 + 8 * g:16 + 8 * g, :], elg)
    m1 = jnp.max(elg, axis=0, keepdims=True)
    i1 = jnp.min(jnp.where(elg == m1, row8, 8), axis=0, keepdims=True)
    elg2 = jnp.where(row8 == i1, NEG_INF, elg)
    m2 = jnp.max(elg2, axis=0, keepdims=True)
    i2 = jnp.min(jnp.where(elg2 == m2, row8, 8), axis=0, keepdims=True)
    e2 = jnp.exp(m2 - m1)
    den = 1.0 + e2
    w0 = (1.0 / den) * pg
    w1 = (e2 / den) * pg
    eid0 = gtop * EXPERTS_PER_GROUP + i1
    eid1 = gtop * EXPERTS_PER_GROUP + i2
    erow = lax.broadcasted_iota(I32, (N_EXPERTS, tm), 0)
    oh0 = erow == eid0
    oh1 = erow == eid1
    ohs = oh0.astype(F32) + oh1.astype(F32)
    base = cnt_ref[...] + _dot(ohs.astype(BF16), tri_ref[...])
    rank0 = jnp.sum(jnp.where(oh0, base, 0.0), axis=0, keepdims=True)
    rank1 = jnp.sum(jnp.where(oh1, base, 0.0), axis=0, keepdims=True)
    cnt_ref[...] = cnt_ref[...] + jnp.sum(ohs, axis=1, keepdims=True)
    cnt_o[...] = cnt_ref[...]
    eid_o[...] = jnp.concatenate([eid0, eid1], axis=0)
    rank_o[...] = jnp.concatenate([rank0, rank1], axis=0).astype(I32)
    wcol_o[...] = jnp.concatenate([w0, w1, jnp.zeros((LANES - 2, tm), F32)], axis=0).T


def _merge(x, ya, yb, ga, gb, mw, tm):
    N, D = x.shape
    tok = lambda w: pl.BlockSpec((tm, w), lambda i: (i, 0))
    rows = lambda r: pl.BlockSpec((r, tm), lambda i: (0, i))
    full = lambda a: pl.BlockSpec(a.shape, lambda i: (0,) * a.ndim)
    tri = jnp.asarray(np.triu(np.ones((tm, tm), np.float32), 1), BF16)
    ws = [mw["woa"], mw["wob"], mw["wout"], mw["gffn"], mw["wrh"], mw["wrl"], mw["br"], tri]
    sds = jax.ShapeDtypeStruct
    return pl.pallas_call(
        _merge_kernel,
        out_shape=[sds((N, D), F32), sds((N, D), F32), sds((2, N), I32), sds((2, N), I32), sds((N, LANES), F32),
                   sds((N_EXPERTS, 1), F32)],
        grid=(N // tm,),
        in_specs=[tok(D), tok(512), tok(512), tok(D), tok(D)] + [full(a) for a in ws],
        out_specs=[tok(D), tok(D), rows(2), rows(2), tok(LANES), pl.BlockSpec((N_EXPERTS, 1), lambda i: (0, 0))],
        scratch_shapes=[pltpu.VMEM((N_EXPERTS, 1), F32)],
        compiler_params=_cparams(("arbitrary",)), name="merge")(x, ya, yb, ga, gb, *ws)


def _merge_weights(w_oa, w_ob, w_out, g_ffn, w_group, b_group, w_expert, b_expert):
    D = w_out.shape[0]
    wr = jnp.concatenate([w_group.T, jnp.zeros((8 - N_GROUPS, D), F32), w_expert.T], axis=0)
    wrh = wr.astype(BF16)
    wrl = (wr - wrh.astype(F32)).astype(BF16)
    br = jnp.concatenate([b_group, jnp.full((8 - N_GROUPS,), NEG_INF, F32), b_expert]).reshape(ROUTER_ROWS, 1)
    return dict(woa=w_oa.astype(BF16), wob=w_ob.astype(BF16), wout=w_out.astype(BF16),
                gffn=g_ffn.reshape(1, D).astype(F32), wrh=wrh, wrl=wrl, br=br.astype(F32))


MOE_ROWS = 256


def _dispatch_kernel(seg_ref, eid_ref, rank_ref, hn_ref, xs_in_ref, xs_ref, sem):
    del xs_in_ref
    tm = hn_ref.shape[0]

    def row_copy(t, k):
        dest = seg_ref[eid_ref[k, t]] + rank_ref[k, t]
        return pltpu.make_async_copy(hn_ref.at[pl.ds(t, 1), :], xs_ref.at[pl.ds(dest, 1), :], sem)

    def start(t, c):
        row_copy(t, 0).start()
        row_copy(t, 1).start()
        return c

    def wait(t, c):
        row_copy(t, 0).wait()
        row_copy(t, 1).wait()
        return c

    lax.fori_loop(0, tm, start, 0, unroll=8)
    lax.fori_loop(0, tm, wait, 0, unroll=8)


def _dispatch(seg_start, eid, rank, hn, n_rows, tm):
    N, D = hn.shape
    smem_rows = pl.BlockSpec((2, tm), lambda i, seg: (0, i), memory_space=pltpu.SMEM)
    grid_spec = pltpu.PrefetchScalarGridSpec(
        num_scalar_prefetch=1, grid=(N // tm,),
        in_specs=[smem_rows, smem_rows, pl.BlockSpec((tm, D), lambda i, seg: (i, 0)),
                  pl.BlockSpec(memory_space=pl.ANY)],
        out_specs=pl.BlockSpec(memory_space=pl.ANY),
        scratch_shapes=[pltpu.SemaphoreType.DMA(())])
    return pl.pallas_call(
        _dispatch_kernel, out_shape=jax.ShapeDtypeStruct((n_rows, D), F32), grid_spec=grid_spec,
        input_output_aliases={4: 0},
        compiler_params=_cparams(("arbitrary",)), name="moe_dispatch",
    )(seg_start, eid, rank, hn, jnp.zeros((n_rows, D), F32))


def _expert_kernel(blk_e_ref, xs_ref, w1_ref, w3_ref, w2_ref, ys_ref):
    del blk_e_ref
    xb = xs_ref[...].astype(BF16)
    hid = jax.nn.silu(_dot(xb, w1_ref[0])) * _dot(xb, w3_ref[0])
    ys_ref[...] = _dot(hid.astype(BF16), w2_ref[0])


def _experts(blk_e, xs, w1, w3, w2):
    P, D = xs.shape
    n_blocks = P // MOE_ROWS
    grid_spec = pltpu.PrefetchScalarGridSpec(
        num_scalar_prefetch=1, grid=(n_blocks,),
        in_specs=[pl.BlockSpec((MOE_ROWS, D), lambda i, e: (i, 0)),
                  pl.BlockSpec((1, D, D_EXPERT), lambda i, e: (e[i], 0, 0)),
                  pl.BlockSpec((1, D, D_EXPERT), lambda i, e: (e[i], 0, 0)),
                  pl.BlockSpec((1, D_EXPERT, D), lambda i, e: (e[i], 0, 0))],
        out_specs=pl.BlockSpec((MOE_ROWS, D), lambda i, e: (i, 0)))
    return pl.pallas_call(
        _expert_kernel, out_shape=jax.ShapeDtypeStruct((P, D), F32), grid_spec=grid_spec,
        compiler_params=_cparams(("arbitrary",)), name="moe_experts")(blk_e, xs, w1, w3, w2)


def _combine_kernel(seg_ref, eid_ref, rank_ref, h_ref, wcol_ref, ys_ref, out_ref, g_ref, sem):
    tm = h_ref.shape[0]

    def row_copy(t, k):
        dest = seg_ref[eid_ref[k, t]] + rank_ref[k, t]
        return pltpu.make_async_copy(ys_ref.at[pl.ds(dest, 1), :], g_ref.at[k, pl.ds(t, 1), :], sem)

    def start(t, c):
        row_copy(t, 0).start()
        row_copy(t, 1).start()
        return c

    def wait(t, c):
        row_copy(t, 0).wait()
        row_copy(t, 1).wait()
        return c

    lax.fori_loop(0, tm, start, 0, unroll=8)
    lax.fori_loop(0, tm, wait, 0, unroll=8)
    w = wcol_ref[...]
    out_ref[...] = h_ref[...] + (g_ref[0] * w[:, 0:1] + g_ref[1] * w[:, 1:2])


def _combine(seg_start, eid, rank, h, wcol, ys, tm):
    N, D = h.shape
    smem_rows = pl.BlockSpec((2, tm), lambda i, seg: (0, i), memory_space=pltpu.SMEM)
    grid_spec = pltpu.PrefetchScalarGridSpec(
        num_scalar_prefetch=1, grid=(N // tm,),
        in_specs=[smem_rows, smem_rows, pl.BlockSpec((tm, D), lambda i, seg: (i, 0)),
                  pl.BlockSpec((tm, LANES), lambda i, seg: (i, 0)), pl.BlockSpec(memory_space=pl.ANY)],
        out_specs=pl.BlockSpec((tm, D), lambda i, seg: (i, 0)),
        scratch_shapes=[pltpu.VMEM((2, tm, D), F32), pltpu.SemaphoreType.DMA(())])
    return pl.pallas_call(
        _combine_kernel, out_shape=jax.ShapeDtypeStruct((N, D), F32), grid_spec=grid_spec,
        compiler_params=_cparams(("arbitrary",)), name="moe_combine")(seg_start, eid, rank, h, wcol, ys)


def _merge_and_ffn(x, ya, yb, ga, gb, mw, ew, tm):
    N, D = x.shape
    h, hn, eid, rank, wcol, counts = _merge(x, ya, yb, ga, gb, mw, tm)
    counts = counts[:, 0].astype(I32)
    padded = (counts + MOE_ROWS - 1) // MOE_ROWS * MOE_ROWS
    seg_end = jnp.cumsum(padded)
    seg_start = (seg_end - padded).astype(I32)
    n_blocks = -(-(2 * N) // MOE_ROWS) + N_EXPERTS
    block_row = jnp.arange(n_blocks, dtype=I32) * MOE_ROWS
    blk_e = jnp.minimum(jnp.sum(seg_end[None, :] <= block_row[:, None], axis=1), N_EXPERTS - 1).astype(I32)
    xs = _dispatch(seg_start, eid, rank, hn, n_blocks * MOE_ROWS, tm)
    ys = _experts(blk_e, xs, ew["w1"], ew["w3"], ew["w2"])
    return _combine(seg_start, eid, rank, h, wcol, ys, tm)


PROJ_TILE = 256


def kernel(x_prompt, x_sample, cache_a_k, cache_a_v, cache_idx_k, cache_mla_latent, cache_mla_krope, page_table,
           rel_bias, g_attn, w_in, b_gate, g_a_q, g_a_k, g_m_qa, w_uq, g_m_qn, g_m_qr, g_m_kva, g_m_kr, w_uk, g_m_kn,
           w_uv, w_oa, w_ob, w_out, g_ffn, w_group, b_group, w_expert, b_expert, w_e1, w_e3, w_e2):
    B, S, D = x_prompt.shape
    l = 0
    pw = _proj_weights(g_attn[l], w_in[l], b_gate[l], g_a_q[l], g_a_k[l], g_m_qa[l], w_uq[l], g_m_qn[l], g_m_qr[l],
                       g_m_kva[l], g_m_kr[l], w_uk[l], g_m_kn[l], w_uv[l])
    pp = _proj(x_prompt, jnp.arange(S, dtype=I32), pw, PROJ_TILE)
    ya_p = _dsa_prompt(pp, _bias_tiles(rel_bias), PROJ_TILE)
    yb_p = _mla_prompt(pp, PROJ_TILE)
    mw = _merge_weights(w_oa[l], w_ob[l], w_out[l], g_ffn[l], w_group[l], b_group[l], w_expert[l], b_expert[l])
    ew = dict(w1=w_e1[l].astype(BF16), w3=w_e3[l].astype(BF16), w2=w_e2[l].astype(BF16))
    flat = lambda a: a.reshape(B * S, a.shape[-1])
    y_p = _merge_and_ffn(flat(x_prompt), flat(ya_p), flat(yb_p), flat(pp["ga"]), flat(pp["gb"]), mw, ew, PROJ_TILE)

    DB = x_sample.shape[0]
    n_pages = page_table.shape[1]
    past = n_pages * PAGE_SIZE
    ps = _proj(x_sample.reshape(1, DB, D), jnp.full((DB,), past, I32), pw, DB)
    ps = {k: v[0] for k, v in ps.items()}
    pt_flat = page_table.reshape(-1).astype(I32)
    heads3 = lambda a: a.reshape(DB, 8, LANES)
    first_row = lambda a: jnp.pad(a[:, None, :], ((0, 0), (0, PAGE_SIZE - 1), (0, 0)))
    first_col = lambda a: jnp.pad(a[:, :, None], ((0, 0), (0, 0), (0, PAGE_SIZE - 1)))
    idx_t = jnp.swapaxes(cache_idx_k[l:l + 1], 2, 3)
    krope_t = jnp.swapaxes(cache_mla_krope[l:l + 1], 2, 3)
    ak_t = jnp.transpose(cache_a_k[l:l + 1], (0, 1, 3, 4, 2))
    av_t = jnp.transpose(cache_a_v[l:l + 1], (0, 1, 3, 4, 2))
    sc, sc_new = _idx_sample(pt_flat, heads3(ps["iqp"]), ps["iwT"], first_col(ps["ik"]), idx_t, n_pages)
    scores = jnp.concatenate([sc.reshape(DB, past), sc_new.reshape(DB, LANES)], axis=1)
    topk = min(TOPK_MAX, (past + 1) // 4)
    sel = _sel_sample(scores.T, topk).T.reshape(DB, n_pages + 1, LANES)
    slot = jnp.arange(PAGE_SIZE, dtype=I32)
    bias2 = jnp.stack([rel_bias[_rel_bucket(PAGE_SIZE - slot)].T,
                       rel_bias[_rel_bucket(2 * PAGE_SIZE - slot)].T]).astype(F32)
    ya_s = _dsa_sample(pt_flat, heads3(ps["aqp"]), sel[:, :n_pages], sel[:, n_pages:], ps["ak"].reshape(DB, 1, 256),
                       ps["av"].reshape(DB, 1, 256), bias2, rel_bias[0].reshape(A_HEADS, 1).astype(F32),
                       ak_t, av_t, n_pages).reshape(DB, A_HEADS * A_HEAD_DIM)
    qgt = (ps["qn"].astype(F32) * jnp.tile(g_m_kn[l], M_HEADS)[None, :]).T
    wukt = w_uk[l].reshape(M_KV_RANK, M_HEADS * M_NOPE).T.astype(BF16)
    yb_s = _mla_sample(pt_flat, qgt, heads3(ps["qr"]), first_row(ps["ckv"]), first_col(ps["kr"]), wukt, pw["wuv"],
                       cache_mla_latent[l:l + 1], krope_t, n_pages).reshape(DB, M_HEADS * M_V)
    y_s = _merge_and_ffn(x_sample.reshape(DB, D), ya_s.astype(BF16), yb_s.astype(BF16), ps["ga"], ps["gb"], mw, ew, DB)

    kv = lambda a, n: a.reshape(1, n, -1, A_KV_HEADS, A_HEAD_DIM)
    return (y_p.reshape(B, S, D), y_s.reshape(DB, 1, D),
            kv(pp["ak"], B), kv(pp["av"], B), pp["ik"][None], pp["ckv"][None], pp["kr"][None],
            kv(ps["ak"], DB), kv(ps["av"], DB), ps["ik"].reshape(1, DB, 1, IDX_DIM),
            ps["ckv"].reshape(1, DB, 1, M_KV_RANK), ps["kr"].reshape(1, DB, 1, M_ROPE))
```
